```python
import math
import jax, jax.numpy as jnp
from jax import lax
import numpy as np

D_MODEL = 2048
BATCH = 16
SEQ = 2048
DEPTH = 2

MEM_LEN = 256
D_FF = 5632
BLK = 128
MLA_HEADS = 8
Q_LORA = 512
KV_LORA = 512
NOPE_DIM = 128
ROPE_DIM = 64
MLA_V_DIM = 128
ROPE_THETA = 10000.0
SGU_CHUNK = 128
SGU_GROUPS = 4
SGU_GROUP_DIM = 128
SGU_WIDTH = SGU_GROUPS * SGU_GROUP_DIM
DIL_PATTERNS = ((128, 1), (512, 4), (2048, 16))
DIL_HEADS_PER_GROUP = 4
DIL_HEAD_DIM = 128
DIL_HEADS = DIL_HEADS_PER_GROUP * len(DIL_PATTERNS)
REL_BUCKETS = 32
REL_MAX_DIST = 2048
XA_HEADS = 4
XA_HEAD_DIM = 128
N_BRANCH = 3
MLA_OUT = MLA_HEADS * MLA_V_DIM
SGU_OUT = SGU_WIDTH
DIL_OUT = DIL_HEADS_PER_GROUP * DIL_HEAD_DIM
MIX_WIDTH = MLA_OUT + SGU_OUT + DIL_OUT
IN_SPLITS = (Q_LORA, KV_LORA, ROPE_DIM, 2 * SGU_WIDTH, 3 * DIL_HEADS * DIL_HEAD_DIM, N_BRANCH * D_MODEL)
N_IN = sum(IN_SPLITS)
ALPHA = (2 * DEPTH) ** 0.25
BETA = (8 * DEPTH) ** -0.25
LN_EPS = 1e-5
RMS_EPS = 1e-6
NEG_INF = -1e30

kernel_name = 'hybrid_mla_sgu_dilated_deepnorm'


def layer_norm(x, g, b):
    xf = x.astype(jnp.float32)
    mu = jnp.mean(xf, -1, keepdims=True)
    var = jnp.mean(jnp.square(xf - mu), -1, keepdims=True)
    return ((xf - mu) * lax.rsqrt(var + LN_EPS)).astype(x.dtype) * g + b


def rms_norm(x, g):
    xf = x.astype(jnp.float32)
    return (xf * lax.rsqrt(jnp.mean(jnp.square(xf), -1, keepdims=True) + RMS_EPS)).astype(x.dtype) * g


def swiglu(x, wg, wu, wd):
    return (jax.nn.silu(x @ wg) * (x @ wu)) @ wd


def rope_tables(seq, dtype):
    inv = ROPE_THETA ** (-jnp.arange(0, ROPE_DIM, 2, dtype=jnp.float32) / ROPE_DIM)
    ang = jnp.arange(seq, dtype=jnp.float32)[:, None] * inv[None, :]
    return jnp.cos(ang).astype(dtype), jnp.sin(ang).astype(dtype)


def apply_rope(t, cos, sin):
    half = t.shape[-1] // 2
    t1, t2 = t[..., :half], t[..., half:]
    c, s = cos[None, :, None, :], sin[None, :, None, :]
    return jnp.concatenate([t1 * c - t2 * s, t1 * s + t2 * c], axis=-1)


def t5_bucket(dist):
    exact = REL_BUCKETS // 2
    df = jnp.maximum(dist, 1).astype(jnp.float32)
    large = exact + (jnp.log(df / exact) / math.log(REL_MAX_DIST / exact) * (REL_BUCKETS - exact)).astype(jnp.int32)
    large = jnp.minimum(large, REL_BUCKETS - 1)
    return jnp.where(dist < exact, dist, large)


def causal_block_attention(q, k, v, scale):
    B, S, H, Dk = q.shape
    nb = S // BLK
    qb = q.reshape(B, nb, BLK, H, Dk).transpose(1, 0, 2, 3, 4)
    kpos = jnp.arange(S)

    def one_block(args):
        qblk, i = args
        s = jnp.einsum('bqhd,bkhd->bhqk', qblk, k).astype(jnp.float32) * scale
        qpos = i * BLK + jnp.arange(BLK)
        s = jnp.where(kpos[None, :] <= qpos[:, None], s, NEG_INF)
        p = jax.nn.softmax(s, axis=-1).astype(v.dtype)
        return jnp.einsum('bhqk,bkhd->bqhd', p, v)

    o = lax.map(one_block, (qb, jnp.arange(nb)))
    return o.transpose(1, 0, 2, 3, 4).reshape(B, S, H, v.shape[-1])


def dilated_group_attention(q, k, v, rel_tab, dil, band):
    B, S, Hg, Dh = q.shape
    L = S // dil
    nb = -(-L // BLK)
    Lp = nb * BLK

    def to_sub(t, front):
        t = t.reshape(B, L, dil, Hg, Dh).transpose(0, 2, 1, 3, 4)
        return jnp.pad(t, ((0, 0), (0, 0), (front, Lp - L), (0, 0), (0, 0)))

    def band_keys(tp):
        prev = tp[:, :, :Lp].reshape(B, dil, nb, BLK, Hg, Dh)
        cur = tp[:, :, BLK:].reshape(B, dil, nb, BLK, Hg, Dh)
        return jnp.concatenate([prev, cur], axis=3)

    qs = to_sub(q, 0).reshape(B, dil, nb, BLK, Hg, Dh)
    kb = band_keys(to_sub(k, BLK))
    vb = band_keys(to_sub(v, BLK))
    qi = jnp.arange(BLK)[:, None]
    ki = jnp.arange(2 * BLK)[None, :]
    j = qi + BLK - ki
    in_band = (j >= 0) & (j <= band)
    bias = rel_tab[t5_bucket(jnp.maximum(j, 0) * dil)].astype(jnp.float32).transpose(2, 0, 1)
    key_real = (jnp.arange(nb)[:, None] > 0) | (ki >= BLK)
    mask = in_band[None] & key_real[:, None, :]
    s = jnp.einsum('brnqhd,brnkhd->brnhqk', qs, kb).astype(jnp.float32) * (DIL_HEAD_DIM ** -0.5) + bias
    s = jnp.where(mask[None, None, :, None], s, NEG_INF)
    lse = jax.nn.logsumexp(s, axis=-1)
    p = jnp.exp(s - lse[..., None]).astype(v.dtype)
    o = jnp.einsum('brnhqk,brnkhd->brnqhd', p, vb)
    o = o.reshape(B, dil, Lp, Hg, Dh)[:, :, :L].transpose(0, 2, 1, 3, 4).reshape(B, S, Hg, Dh)
    lse = lse.transpose(0, 1, 2, 4, 3).reshape(B, dil, Lp, Hg)[:, :, :L].transpose(0, 2, 1, 3).reshape(B, S, Hg)
    return o, lse


def mla_branch(c_q, c_kv, k_pe, q_norm, kv_norm, w_uq, w_ukv, cos, sin):
    B, S, _ = c_q.shape
    q = (rms_norm(c_q, q_norm) @ w_uq).reshape(B, S, MLA_HEADS, NOPE_DIM + ROPE_DIM)
    q = jnp.concatenate([q[..., :NOPE_DIM], apply_rope(q[..., NOPE_DIM:], cos, sin)], axis=-1)
    kv = (rms_norm(c_kv, kv_norm) @ w_ukv).reshape(B, S, MLA_HEADS, NOPE_DIM + MLA_V_DIM)
    k_rope = jnp.broadcast_to(apply_rope(k_pe[:, :, None, :], cos, sin), (B, S, MLA_HEADS, ROPE_DIM))
    k = jnp.concatenate([kv[..., :NOPE_DIM], k_rope], axis=-1)
    o = causal_block_attention(q, k, kv[..., NOPE_DIM:], (NOPE_DIM + ROPE_DIM) ** -0.5)
    return o.reshape(B, S, MLA_OUT)


def spatial_gating(z, ln_g, ln_b, ws, bs):
    B, S, _ = z.shape
    u, v = jnp.split(jax.nn.gelu(z, approximate=False), 2, axis=-1)
    v = layer_norm(v, ln_g, ln_b).reshape(B, S // SGU_CHUNK, SGU_CHUNK, SGU_GROUPS, SGU_GROUP_DIM)
    w = ws * jnp.tril(jnp.ones((SGU_CHUNK, SGU_CHUNK), ws.dtype))
    mixed = jnp.einsum('gts,bcsgd->bctgd', w, v) + bs.T[:, :, None]
    return u * mixed.reshape(B, S, SGU_WIDTH)


def dilated_branch(qkv, rel_bias):
    B, S, _ = qkv.shape
    qkv = qkv.reshape(B, S, 3, DIL_HEADS, DIL_HEAD_DIM)
    outs, lses = [], []
    for g, (window, dil) in enumerate(DIL_PATTERNS):
        hs = slice(g * DIL_HEADS_PER_GROUP, (g + 1) * DIL_HEADS_PER_GROUP)
        o, lse = dilated_group_attention(qkv[:, :, 0, hs], qkv[:, :, 1, hs], qkv[:, :, 2, hs],
                                         rel_bias[:, hs], dil, window // dil)
        outs.append(o)
        lses.append(lse)
    wts = jax.nn.softmax(jnp.stack(lses, 0), axis=0).astype(qkv.dtype)
    o = jnp.sum(wts[..., None] * jnp.stack(outs, 0), axis=0)
    return o.reshape(B, S, DIL_OUT)


def hybrid_mixer(h, w_in, q_norm, kv_norm, w_uq, w_ukv, sgu_g, sgu_b, ws, bs, rel_bias, w_branch, w_out, cos, sin):
    B, S, _ = h.shape
    offs = np.cumsum(IN_SPLITS)[:-1].tolist()
    c_q, c_kv, k_pe, z, qkv, gate_in = jnp.split(h @ w_in, offs, axis=-1)
    o_a = mla_branch(c_q, c_kv, k_pe, q_norm, kv_norm, w_uq, w_ukv, cos, sin)
    o_b = spatial_gating(z, sgu_g, sgu_b, ws, bs)
    o_c = dilated_branch(qkv, rel_bias)
    gates = jax.nn.sigmoid(gate_in).reshape(B, S, N_BRANCH, D_MODEL)
    p_a, p_b, p_c = jnp.split(w_branch, [MLA_OUT, MLA_OUT + SGU_OUT], axis=0)
    merged = gates[:, :, 0] * (o_a @ p_a) + gates[:, :, 1] * (o_b @ p_b) + gates[:, :, 2] * (o_c @ p_c)
    return merged @ w_out


def memory_cross_attention(h, mem, wq, wkv, wo):
    B, S, _ = h.shape
    M = mem.shape[1]
    q = (h @ wq).reshape(B, S, XA_HEADS, XA_HEAD_DIM)
    kv = (mem @ wkv).reshape(B, M, 2, XA_HEADS, XA_HEAD_DIM)
    s = jnp.einsum('bqhd,bkhd->bhqk', q, kv[:, :, 0]).astype(jnp.float32) * (XA_HEAD_DIM ** -0.5)
    p = jax.nn.softmax(s, axis=-1).astype(h.dtype)
    o = jnp.einsum('bhqk,bkhd->bqhd', p, kv[:, :, 1]).reshape(B, S, XA_HEADS * XA_HEAD_DIM)
    return o @ wo


def setup_inputs(seed: int = 0) -> dict:
    key = jax.random.key(seed)
    ks = jax.random.split(key, 24)
    D, F = D_MODEL, D_FF

    def nrm(k, shape, fan_in, gain=1.0):
        return jax.random.normal(k, shape, jnp.float32) * (gain * fan_in ** -0.5)

    def noise(k, shape, scale=0.02):
        return jax.random.normal(k, shape, jnp.float32) * scale

    kb1, kb2, kb3 = jax.random.split(ks[17], 3)
    w_branch = jnp.concatenate([nrm(kb1, (DEPTH, MLA_OUT, D), MLA_OUT),
                                nrm(kb2, (DEPTH, SGU_OUT, D), SGU_OUT),
                                nrm(kb3, (DEPTH, DIL_OUT, D), DIL_OUT)], axis=1)
    return {
        'x': jax.random.normal(ks[0], (BATCH, SEQ, D), jnp.float32),
        'mem': jax.random.normal(ks[1], (BATCH, MEM_LEN, D), jnp.float32),
        'ln_g': 1.0 + noise(ks[2], (DEPTH, 4, D)),
        'ln_b': noise(ks[3], (DEPTH, 4, D)),
        'ffn_wg': nrm(ks[4], (DEPTH, 2, D, F), D),
        'ffn_wu': nrm(ks[5], (DEPTH, 2, D, F), D),
        'ffn_wd': nrm(ks[6], (DEPTH, 2, F, D), F, BETA),
        'w_in': nrm(ks[7], (DEPTH, D, N_IN), D),
        'mla_q_norm': 1.0 + noise(ks[8], (DEPTH, Q_LORA)),
        'mla_kv_norm': 1.0 + noise(ks[9], (DEPTH, KV_LORA)),
        'mla_w_uq': nrm(ks[10], (DEPTH, Q_LORA, MLA_HEADS * (NOPE_DIM + ROPE_DIM)), Q_LORA),
        'mla_w_ukv': nrm(ks[11], (DEPTH, KV_LORA, MLA_HEADS * (NOPE_DIM + MLA_V_DIM)), KV_LORA),
        'sgu_ln_g': 1.0 + noise(ks[12], (DEPTH, SGU_WIDTH)),
        'sgu_ln_b': noise(ks[13], (DEPTH, SGU_WIDTH)),
        'sgu_ws': nrm(ks[14], (DEPTH, SGU_GROUPS, SGU_CHUNK, SGU_CHUNK), SGU_CHUNK),
        'sgu_bs': 1.0 + noise(ks[15], (DEPTH, SGU_GROUPS, SGU_CHUNK)),
        'rel_bias': noise(ks[16], (REL_BUCKETS, DIL_HEADS), 0.5),
        'w_branch': w_branch,
        'w_out': nrm(ks[18], (DEPTH, D, D), D, BETA),
        'xa_wq': nrm(ks[19], (DEPTH, D, XA_HEADS * XA_HEAD_DIM), D),
        'xa_wkv': nrm(ks[20], (DEPTH, D, 2 * XA_HEADS * XA_HEAD_DIM), D),
        'xa_wo': nrm(ks[21], (DEPTH, XA_HEADS * XA_HEAD_DIM, D), XA_HEADS * XA_HEAD_DIM, BETA),
    }


def reference(x, mem, ln_g, ln_b, ffn_wg, ffn_wu, ffn_wd, w_in, mla_q_norm, mla_kv_norm, mla_w_uq, mla_w_ukv,
              sgu_ln_g, sgu_ln_b, sgu_ws, sgu_bs, rel_bias, w_branch, w_out, xa_wq, xa_wkv, xa_wo):
    cos, sin = rope_tables(x.shape[1], x.dtype)
    for l in range(DEPTH):
        x = layer_norm(ALPHA * x + 0.5 * swiglu(x, ffn_wg[l, 0], ffn_wu[l, 0], ffn_wd[l, 0]), ln_g[l, 0], ln_b[l, 0])
        y = hybrid_mixer(x, w_in[l], mla_q_norm[l], mla_kv_norm[l], mla_w_uq[l], mla_w_ukv[l],
                         sgu_ln_g[l], sgu_ln_b[l], sgu_ws[l], sgu_bs[l], rel_bias, w_branch[l], w_out[l], cos, sin)
        x = layer_norm(ALPHA * x + y, ln_g[l, 1], ln_b[l, 1])
        x = layer_norm(ALPHA * x + memory_cross_attention(x, mem, xa_wq[l], xa_wkv[l], xa_wo[l]), ln_g[l, 2], ln_b[l, 2])
        x = layer_norm(ALPHA * x + 0.5 * swiglu(x, ffn_wg[l, 1], ffn_wu[l, 1], ffn_wd[l, 1]), ln_g[l, 3], ln_b[l, 3])
    return x
```

```python
import functools
import math

import jax
import jax.numpy as jnp
import numpy as np
from jax import lax
from jax.experimental import pallas as pl
from jax.experimental.pallas import tpu as pltpu

F32 = jnp.float32
BF16 = jnp.bfloat16

D_MODEL = 2048
DEPTH = 2
D_FF = 5632
MLA_HEADS = 8
Q_LORA = 512
KV_LORA = 512
NOPE_DIM = 128
ROPE_DIM = 64
MLA_V_DIM = 128
MLA_QK_PAD = 256
ROPE_THETA = 10000.0
SGU_CHUNK = 128
SGU_GROUPS = 4
SGU_GROUP_DIM = 128
SGU_WIDTH = SGU_GROUPS * SGU_GROUP_DIM
DIL_PATTERNS = ((128, 1), (512, 4), (2048, 16))
DIL_HEADS_PER_GROUP = 4
DIL_HEAD_DIM = 128
DIL_HEADS = DIL_HEADS_PER_GROUP * len(DIL_PATTERNS)
DIL_OUT = DIL_HEADS_PER_GROUP * DIL_HEAD_DIM
DIL_BLK = 128
REL_BUCKETS = 32
REL_MAX_DIST = 2048
XA_HEADS = 4
XA_HEAD_DIM = 128
XA_WIDTH = XA_HEADS * XA_HEAD_DIM
MLA_OUT = MLA_HEADS * MLA_V_DIM
ALPHA = (2 * DEPTH) ** 0.25
LN_EPS = 1e-5
RMS_EPS = 1e-6
NEG_INF = -1e30

VMEM_LIMIT_BYTES = 56 * 1024 * 1024


def _params(*semantics):
    return pltpu.CompilerParams(dimension_semantics=semantics, vmem_limit_bytes=VMEM_LIMIT_BYTES)


def _ln_rows(y, g, b):
    mu = jnp.mean(y, axis=-1, keepdims=True)
    yc = y - mu
    var = jnp.mean(yc * yc, axis=-1, keepdims=True)
    return yc * lax.rsqrt(var + LN_EPS) * g + b


def _rms_rows(y, g):
    return y * lax.rsqrt(jnp.mean(y * y, axis=-1, keepdims=True) + RMS_EPS) * g


def _dot(a, b):
    return jnp.dot(a, b, preferred_element_type=F32)


def _dot_nt(a, b):
    return lax.dot_general(a, b, (((1,), (1,)), ((), ())), preferred_element_type=F32)


def _ffn_kernel(x_ref, wg_ref, wu_ref, wd_ref, g_ref, b_ref, o_ref, xb_ref):
    f = pl.program_id(1)

    @pl.when(f == 0)
    def _():
        x = x_ref[...]
        xb_ref[...] = x.astype(BF16)
        o_ref[...] = ALPHA * x

    xb = xb_ref[...]
    gate = _dot(xb, wg_ref[...])
    up = _dot(xb, wu_ref[...])
    h = (0.5 * (gate * jax.nn.sigmoid(gate)) * up).astype(BF16)
    o_ref[...] += _dot(h, wd_ref[...])

    @pl.when(f == pl.num_programs(1) - 1)
    def _():
        o_ref[...] = _ln_rows(o_ref[...], g_ref[...], b_ref[...])


def _ffn(x, wg, wu, wd, g, b, tm=512, tf=512):
    t, d = x.shape
    ff = wg.shape[1]
    return pl.pallas_call(
        _ffn_kernel,
        grid=(t // tm, ff // tf),
        in_specs=[
            pl.BlockSpec((tm, d), lambda i, f: (i, 0)),
            pl.BlockSpec((d, tf), lambda i, f: (0, f)),
            pl.BlockSpec((d, tf), lambda i, f: (0, f)),
            pl.BlockSpec((tf, d), lambda i, f: (f, 0)),
            pl.BlockSpec((1, d), lambda i, f: (0, 0)),
            pl.BlockSpec((1, d), lambda i, f: (0, 0)),
        ],
        out_specs=pl.BlockSpec((tm, d), lambda i, f: (i, 0)),
        out_shape=jax.ShapeDtypeStruct((t, d), F32),
        scratch_shapes=[pltpu.VMEM((tm, d), BF16)],
        compiler_params=_params("parallel", "arbitrary"),
        name="ffn_swiglu_ln",
    )(x, wg, wu, wd, g, b)


def _proj_kernel(x_ref, w_ref, o_ref):
    o_ref[...] = _dot(x_ref[...].astype(BF16), w_ref[...]).astype(o_ref.dtype)


def _proj(x, w, out_dtype, tm, tn):
    t, k = x.shape
    n = w.shape[1]
    tm = min(tm, t)
    assert t % tm == 0 and n % tn == 0
    return pl.pallas_call(
        _proj_kernel,
        grid=(t // tm, n // tn),
        in_specs=[
            pl.BlockSpec((tm, k), lambda i, j: (i, 0)),
            pl.BlockSpec((k, tn), lambda i, j: (0, j)),
        ],
        out_specs=pl.BlockSpec((tm, tn), lambda i, j: (i, j)),
        out_shape=jax.ShapeDtypeStruct((t, n), out_dtype),
        compiler_params=_params("parallel", "arbitrary"),
        name="proj",
    )(x, w)


def _mla_proj_kernel(x_ref, w1_ref, qn_ref, kvn_ref, wqn_ref, wqr_ref, wqrr_ref, wk_ref, wv_ref,
                     cos_ref, sin_ref, q_ref, k_ref, v_ref):
    xb = x_ref[...].astype(BF16)
    c = _dot(xb, w1_ref[...])
    cos = cos_ref[...]
    sin = sin_ref[...]
    cq = _rms_rows(c[:, :Q_LORA], qn_ref[...]).astype(BF16)
    ckv = _rms_rows(c[:, Q_LORA:Q_LORA + KV_LORA], kvn_ref[...]).astype(BF16)
    off = Q_LORA + KV_LORA
    k_rope = (c[:, off:off + 128] * cos + c[:, off + 128:off + 256] * sin).astype(BF16)
    q_nope = _dot(cq, wqn_ref[...])
    q_pe = _dot(cq, wqr_ref[...])
    q_pe_rot = _dot(cq, wqrr_ref[...])
    k_nope = _dot(ckv, wk_ref[...])
    v_ref[...] = _dot(ckv, wv_ref[...]).astype(BF16)
    for h in range(MLA_HEADS):
        lo = h * MLA_QK_PAD
        hs = slice(h * 128, (h + 1) * 128)
        q_ref[:, lo:lo + 128] = q_nope[:, hs].astype(BF16)
        q_ref[:, lo + 128:lo + 256] = (q_pe[:, hs] * cos + q_pe_rot[:, hs] * sin).astype(BF16)
        k_ref[:, lo:lo + 128] = k_nope[:, hs].astype(BF16)
        k_ref[:, lo + 128:lo + 256] = k_rope


def _mla_proj(x, w1, qn, kvn, wqn, wqr, wqrr, wk, wv, cos128, sin128, seq, tm=512):
    t, d = x.shape
    nseq = seq // tm
    full = lambda a: pl.BlockSpec(a.shape, lambda i: (0,) * a.ndim)
    hq = MLA_HEADS * MLA_QK_PAD
    return pl.pallas_call(
        _mla_proj_kernel,
        grid=(t // tm,),
        in_specs=[
            pl.BlockSpec((tm, d), lambda i: (i, 0)),
            full(w1), full(qn), full(kvn), full(wqn), full(wqr), full(wqrr), full(wk), full(wv),
            pl.BlockSpec((tm, 128), lambda i: (i % nseq, 0)),
            pl.BlockSpec((tm, 128), lambda i: (i % nseq, 0)),
        ],
        out_specs=[
            pl.BlockSpec((tm, hq), lambda i: (i, 0)),
            pl.BlockSpec((tm, hq), lambda i: (i, 0)),
            pl.BlockSpec((tm, MLA_OUT), lambda i: (i, 0)),
        ],
        out_shape=[
            jax.ShapeDtypeStruct((t, hq), BF16),
            jax.ShapeDtypeStruct((t, hq), BF16),
            jax.ShapeDtypeStruct((t, MLA_OUT), BF16),
        ],
        compiler_params=_params("parallel"),
        name="mla_proj",
    )(x, w1, qn, kvn, wqn, wqr, wqrr, wk, wv, cos128, sin128)


def _mla_attn_kernel(q_ref, k_ref, v_ref, o_ref, *, tq, scale):
    seq = q_ref.shape[1]
    for qi in range(seq // tq):
        kv_len = (qi + 1) * tq
        q = q_ref[0, qi * tq:(qi + 1) * tq, :]
        s = _dot_nt(q, k_ref[0, :kv_len, :]) * scale
        qpos = qi * tq + lax.broadcasted_iota(jnp.int32, (tq, kv_len), 0)
        kpos = lax.broadcasted_iota(jnp.int32, (tq, kv_len), 1)
        s = jnp.where(kpos <= qpos, s, NEG_INF)
        m = jnp.max(s, axis=-1, keepdims=True)
        e = jnp.exp(s - m)
        p = (e * (1.0 / jnp.sum(e, axis=-1, keepdims=True))).astype(BF16)
        o_ref[0, qi * tq:(qi + 1) * tq, :] = _dot(p, v_ref[0, :kv_len, :]).astype(o_ref.dtype)


def _mla_attn(q, k, v, tq=256):
    b, s, _ = q.shape
    kern = functools.partial(_mla_attn_kernel, tq=tq, scale=(NOPE_DIM + ROPE_DIM) ** -0.5)
    return pl.pallas_call(
        kern,
        grid=(b, MLA_HEADS),
        in_specs=[
            pl.BlockSpec((1, s, MLA_QK_PAD), lambda i, h: (i, 0, h)),
            pl.BlockSpec((1, s, MLA_QK_PAD), lambda i, h: (i, 0, h)),
            pl.BlockSpec((1, s, MLA_V_DIM), lambda i, h: (i, 0, h)),
        ],
        out_specs=pl.BlockSpec((1, s, MLA_V_DIM), lambda i, h: (i, 0, h)),
        out_shape=jax.ShapeDtypeStruct((b, s, MLA_OUT), BF16),
        compiler_params=_params("parallel", "parallel"),
        name="mla_attn",
    )(q, k, v)


def _sgu_kernel(x_ref, wz_ref, g_ref, b_ref, ws_ref, bs_ref, o_ref):
    tm = x_ref.shape[0]
    z = _dot(x_ref[...].astype(BF16), wz_ref[...])
    z = 0.5 * z * (1.0 + lax.erf(z * (2.0 ** -0.5)))
    u = z[:, :SGU_WIDTH]
    vn = _ln_rows(z[:, SGU_WIDTH:], g_ref[...], b_ref[...]).astype(BF16)
    row = lax.broadcasted_iota(jnp.int32, (SGU_CHUNK, SGU_CHUNK), 0)
    col = lax.broadcasted_iota(jnp.int32, (SGU_CHUNK, SGU_CHUNK), 1)
    for gi in range(SGU_GROUPS):
        w = jnp.where(col <= row, ws_ref[gi], 0.0).astype(BF16)
        bias = bs_ref[gi]
        cs = slice(gi * SGU_GROUP_DIM, (gi + 1) * SGU_GROUP_DIM)
        for ci in range(tm // SGU_CHUNK):
            rs = slice(ci * SGU_CHUNK, (ci + 1) * SGU_CHUNK)
            mixed = _dot(w, vn[rs, cs]) + bias
            o_ref[rs, cs] = (u[rs, cs] * mixed).astype(o_ref.dtype)


def _sgu(x, wz, g, b, ws, bs_b, tm=512):
    t, d = x.shape
    full = lambda a: pl.BlockSpec(a.shape, lambda i: (0,) * a.ndim)
    return pl.pallas_call(
        _sgu_kernel,
        grid=(t // tm,),
        in_specs=[pl.BlockSpec((tm, d), lambda i: (i, 0)), full(wz), full(g), full(b), full(ws), full(bs_b)],
        out_specs=pl.BlockSpec((tm, SGU_WIDTH), lambda i: (i, 0)),
        out_shape=jax.ShapeDtypeStruct((t, SGU_WIDTH), BF16),
        compiler_params=_params("parallel"),
        name="sgu",
    )(x, wz, g, b, ws, bs_b)


def _dil_attn_kernel(q_ref, k_ref, v_ref, bias_ref, o_ref, l_ref, *, band, scale):
    sub = q_ref.shape[2]
    qi = lax.broadcasted_iota(jnp.int32, (DIL_BLK, 2 * DIL_BLK), 0)
    ki = lax.broadcasted_iota(jnp.int32, (DIL_BLK, 2 * DIL_BLK), 1)
    dist = qi + DIL_BLK - ki
    in_band = (dist >= 0) & (dist <= band)
    for h in range(DIL_HEADS_PER_GROUP):
        cs = slice(h * DIL_HEAD_DIM, (h + 1) * DIL_HEAD_DIM)
        bias = bias_ref[h]
        for n in range(sub // DIL_BLK):
            rs = slice(n * DIL_BLK, (n + 1) * DIL_BLK)
            q = q_ref[0, 0, rs, cs]
            if n == 0:
                s = _dot_nt(q, k_ref[0, 0, rs, cs]) * scale + bias[:, DIL_BLK:]
                s = jnp.where(in_band[:, DIL_BLK:], s, NEG_INF)
                vv = v_ref[0, 0, rs, cs]
            else:
                ks = slice((n - 1) * DIL_BLK, (n + 1) * DIL_BLK)
                s = _dot_nt(q, k_ref[0, 0, ks, cs]) * scale + bias
                s = jnp.where(in_band, s, NEG_INF)
                vv = v_ref[0, 0, ks, cs]
            m = jnp.max(s, axis=-1, keepdims=True)
            e = jnp.exp(s - m)
            den = jnp.sum(e, axis=-1, keepdims=True)
            p = (e * (1.0 / den)).astype(BF16)
            o_ref[0, 0, rs, cs] = _dot(p, vv)
            l_ref[0, 0, rs, cs] = jnp.broadcast_to(m + jnp.log(den), (DIL_BLK, DIL_HEAD_DIM))


def _dil_attn(q, k, v, bias, band):
    b, dil, sub, w = q.shape
    kern = functools.partial(_dil_attn_kernel, band=band, scale=DIL_HEAD_DIM ** -0.5)
    blk = pl.BlockSpec((1, 1, sub, w), lambda i, r: (i, r, 0, 0))
    return pl.pallas_call(
        kern,
        grid=(b, dil),
        in_specs=[blk, blk, blk, pl.BlockSpec(bias.shape, lambda i, r: (0, 0, 0))],
        out_specs=[blk, blk],
        out_shape=[jax.ShapeDtypeStruct((b, dil, sub, w), F32)] * 2,
        compiler_params=_params("parallel", "parallel"),
        name="dil_attn",
    )(q, k, v, bias)


def _dil_combine_kernel(o0, o1, o2, l0, l1, l2, out_ref):
    a0, a1, a2 = l0[...], l1[...], l2[...]
    m = jnp.maximum(jnp.maximum(a0, a1), a2)
    e0, e1, e2 = jnp.exp(a0 - m), jnp.exp(a1 - m), jnp.exp(a2 - m)
    inv = 1.0 / (e0 + e1 + e2)
    out_ref[...] = ((e0 * inv) * o0[...] + (e1 * inv) * o1[...] + (e2 * inv) * o2[...]).astype(out_ref.dtype)


def _dil_combine(outs, lses, tm=1024):
    t, w = outs[0].shape
    blk = pl.BlockSpec((tm, w), lambda i: (i, 0))
    return pl.pallas_call(
        _dil_combine_kernel,
        grid=(t // tm,),
        in_specs=[blk] * 6,
        out_specs=blk,
        out_shape=jax.ShapeDtypeStruct((t, w), BF16),
        compiler_params=_params("parallel"),
        name="dil_combine",
    )(*outs, *lses)


def _merge_kernel(x_ref, oa_ref, ob_ref, oc_ref, wga_ref, wgb_ref, wgc_ref, pa_ref, pb_ref, pc_ref,
                  wo_ref, g_ref, b_ref, o_ref, xb_ref):
    j = pl.program_id(1)

    @pl.when(j == 0)
    def _():
        x = x_ref[...]
        xb_ref[...] = x.astype(BF16)
        o_ref[...] = ALPHA * x

    xb = xb_ref[...]
    merged = jax.nn.sigmoid(_dot(xb, wga_ref[...])) * _dot(oa_ref[...], pa_ref[...])
    merged += jax.nn.sigmoid(_dot(xb, wgb_ref[...])) * _dot(ob_ref[...], pb_ref[...])
    merged += jax.nn.sigmoid(_dot(xb, wgc_ref[...])) * _dot(oc_ref[...], pc_ref[...])
    o_ref[...] += _dot(merged.astype(BF16), wo_ref[...])

    @pl.when(j == pl.num_programs(1) - 1)
    def _():
        o_ref[...] = _ln_rows(o_ref[...], g_ref[...], b_ref[...])


def _merge(x, oa, ob, oc, wga, wgb, wgc, pa, pb, pc, wo, g, b, tm=512, tn=512):
    t, d = x.shape
    rows = lambda a: pl.BlockSpec((tm, a.shape[1]), lambda i, j: (i, 0))
    cols = lambda a: pl.BlockSpec((a.shape[0], tn), lambda i, j: (0, j))
    return pl.pallas_call(
        _merge_kernel,
        grid=(t // tm, d // tn),
        in_specs=[rows(x), rows(oa), rows(ob), rows(oc), cols(wga), cols(wgb), cols(wgc), cols(pa), cols(pb),
                  cols(pc), pl.BlockSpec((tn, d), lambda i, j: (j, 0)),
                  pl.BlockSpec((1, d), lambda i, j: (0, 0)), pl.BlockSpec((1, d), lambda i, j: (0, 0))],
        out_specs=pl.BlockSpec((tm, d), lambda i, j: (i, 0)),
        out_shape=jax.ShapeDtypeStruct((t, d), F32),
        scratch_shapes=[pltpu.VMEM((tm, d), BF16)],
        compiler_params=_params("parallel", "arbitrary"),
        name="merge_out_ln",
    )(x, oa, ob, oc, wga, wgb, wgc, pa, pb, pc, wo, g, b)


def _xa_kernel(x_ref, kv_ref, wq_ref, wo_ref, g_ref, b_ref, o_ref):
    x = x_ref[...]
    q = _dot(x.astype(BF16), wq_ref[...]).astype(BF16)
    heads = []
    for h in range(XA_HEADS):
        cs = slice(h * XA_HEAD_DIM, (h + 1) * XA_HEAD_DIM)
        s = _dot_nt(q[:, cs], kv_ref[0, :, cs]) * (XA_HEAD_DIM ** -0.5)
        m = jnp.max(s, axis=-1, keepdims=True)
        e = jnp.exp(s - m)
        p = (e * (1.0 / jnp.sum(e, axis=-1, keepdims=True))).astype(BF16)
        heads.append(_dot(p, kv_ref[0, :, XA_WIDTH + h * XA_HEAD_DIM:XA_WIDTH + (h + 1) * XA_HEAD_DIM]))
    o = jnp.concatenate(heads, axis=-1).astype(BF16)
    o_ref[...] = _ln_rows(ALPHA * x + _dot(o, wo_ref[...]), g_ref[...], b_ref[...])


def _xa(x, kv, wq, wo, g, b, seq, tm=512):
    t, d = x.shape
    nseq = seq // tm
    full = lambda a: pl.BlockSpec(a.shape, lambda i: (0,) * a.ndim)
    return pl.pallas_call(
        _xa_kernel,
        grid=(t // tm,),
        in_specs=[pl.BlockSpec((tm, d), lambda i: (i, 0)),
                  pl.BlockSpec((1,) + kv.shape[1:], lambda i: (i // nseq, 0, 0)),
                  full(wq), full(wo), full(g), full(b)],
        out_specs=pl.BlockSpec((tm, d), lambda i: (i, 0)),
        out_shape=jax.ShapeDtypeStruct((t, d), F32),
        compiler_params=_params("parallel"),
        name="xattn_ln",
    )(x, kv, wq, wo, g, b)


def _rope_tables128(seq):
    inv = ROPE_THETA ** (-jnp.arange(0, ROPE_DIM, 2, dtype=F32) / ROPE_DIM)
    ang = jnp.arange(seq, dtype=F32)[:, None] * inv[None, :]
    z = jnp.zeros((seq, 128 - ROPE_DIM), F32)
    cos, sin = jnp.cos(ang), jnp.sin(ang)
    return jnp.concatenate([cos, cos, z], axis=1), jnp.concatenate([sin, sin, z], axis=1)


def _rot_cols(w):
    half = w.shape[-1] // 2
    return jnp.concatenate([-w[..., half:], w[..., :half]], axis=-1)


def _pad_last(w, width):
    return jnp.pad(w, [(0, 0)] * (w.ndim - 1) + [(0, width - w.shape[-1])])


def _t5_bucket(dist):
    exact = REL_BUCKETS // 2
    df = jnp.maximum(dist, 1).astype(F32)
    large = exact + (jnp.log(df / exact) / math.log(REL_MAX_DIST / exact) * (REL_BUCKETS - exact)).astype(jnp.int32)
    large = jnp.minimum(large, REL_BUCKETS - 1)
    return jnp.where(dist < exact, dist, large)


def _dil_bias(rel_tab, dil):
    qi = jnp.arange(DIL_BLK)[:, None]
    ki = jnp.arange(2 * DIL_BLK)[None, :]
    dist = qi + DIL_BLK - ki
    return rel_tab[_t5_bucket(jnp.maximum(dist, 0) * dil)].astype(F32).transpose(2, 0, 1)


def _to_sub(t, batch, seq, dil):
    w = t.shape[-1]
    return t.reshape(batch, seq // dil, dil, w).transpose(0, 2, 1, 3)


def _from_sub(t, batch, seq, dil):
    w = t.shape[-1]
    return t.transpose(0, 2, 1, 3).reshape(batch * seq, w)


def kernel(x, mem, ln_g, ln_b, ffn_wg, ffn_wu, ffn_wd, w_in, mla_q_norm, mla_kv_norm, mla_w_uq, mla_w_ukv,
           sgu_ln_g, sgu_ln_b, sgu_ws, sgu_bs, rel_bias, w_branch, w_out, xa_wq, xa_wkv, xa_wo):
    batch, seq, d = x.shape
    t = batch * seq
    mem_len = mem.shape[1]
    cos128, sin128 = _rope_tables128(seq)
    h = x.reshape(t, d)
    mem2 = mem.reshape(batch * mem_len, d)
    bf = lambda a: a.astype(BF16)
    row = lambda a: a.reshape(1, -1)

    o_cq = 0
    o_ckv = o_cq + Q_LORA
    o_kpe = o_ckv + KV_LORA
    o_z = o_kpe + ROPE_DIM
    o_qkv = o_z + 2 * SGU_WIDTH
    o_gate = o_qkv + 3 * DIL_HEADS * DIL_HEAD_DIM

    for l in range(DEPTH):
        h = _ffn(h, bf(ffn_wg[l, 0]), bf(ffn_wu[l, 0]), bf(ffn_wd[l, 0]), row(ln_g[l, 0]), row(ln_b[l, 0]))

        wl = w_in[l]
        w_kpe = wl[:, o_kpe:o_z]
        w1 = bf(jnp.concatenate([wl[:, o_cq:o_kpe], _pad_last(w_kpe, 128), _pad_last(_rot_cols(w_kpe), 128)], axis=1))
        wuq = mla_w_uq[l].reshape(Q_LORA, MLA_HEADS, NOPE_DIM + ROPE_DIM)
        wqn = bf(wuq[:, :, :NOPE_DIM].reshape(Q_LORA, -1))
        wqr = bf(_pad_last(wuq[:, :, NOPE_DIM:], 128).reshape(Q_LORA, -1))
        wqrr = bf(_pad_last(_rot_cols(wuq[:, :, NOPE_DIM:]), 128).reshape(Q_LORA, -1))
        wukv = mla_w_ukv[l].reshape(KV_LORA, MLA_HEADS, NOPE_DIM + MLA_V_DIM)
        wk = bf(wukv[:, :, :NOPE_DIM].reshape(KV_LORA, -1))
        wv = bf(wukv[:, :, NOPE_DIM:].reshape(KV_LORA, -1))
        q, k, v = _mla_proj(h, w1, row(mla_q_norm[l]), row(mla_kv_norm[l]), wqn, wqr, wqrr, wk, wv, cos128, sin128, seq)
        hq = MLA_HEADS * MLA_QK_PAD
        o_a = _mla_attn(q.reshape(batch, seq, hq), k.reshape(batch, seq, hq), v.reshape(batch, seq, MLA_OUT))
        o_a = o_a.reshape(t, MLA_OUT)

        bs_b = jnp.broadcast_to(sgu_bs[l][:, :, None], (SGU_GROUPS, SGU_CHUNK, SGU_GROUP_DIM))
        o_b = _sgu(h, bf(wl[:, o_z:o_qkv]), row(sgu_ln_g[l]), row(sgu_ln_b[l]), sgu_ws[l], bs_b)

        qkv = _proj(h, bf(wl[:, o_qkv:o_gate]), BF16, tm=1024, tn=DIL_HEADS * DIL_HEAD_DIM)
        qkv = qkv.reshape(t, 3, len(DIL_PATTERNS), DIL_OUT)
        outs, lses = [], []
        for gi, (window, dil) in enumerate(DIL_PATTERNS):
            hs = slice(gi * DIL_HEADS_PER_GROUP, (gi + 1) * DIL_HEADS_PER_GROUP)
            qs, ks, vs = (_to_sub(qkv[:, c, gi], batch, seq, dil) for c in range(3))
            o_g, l_g = _dil_attn(qs, ks, vs, _dil_bias(rel_bias[:, hs], dil), window // dil)
            outs.append(_from_sub(o_g, batch, seq, dil))
            lses.append(_from_sub(l_g, batch, seq, dil))
        o_c = _dil_combine(outs, lses)

        wg = wl[:, o_gate:]
        p = w_branch[l]
        h = _merge(h, o_a, o_b, o_c, bf(wg[:, :d]), bf(wg[:, d:2 * d]), bf(wg[:, 2 * d:]),
                   bf(p[:MLA_OUT]), bf(p[MLA_OUT:MLA_OUT + SGU_WIDTH]), bf(p[MLA_OUT + SGU_WIDTH:]),
                   bf(w_out[l]), row(ln_g[l, 1]), row(ln_b[l, 1]))

        kv = _proj(mem2, bf(xa_wkv[l]), BF16, tm=1024, tn=2 * XA_WIDTH).reshape(batch, mem_len, 2 * XA_WIDTH)
        h = _xa(h, kv, bf(xa_wq[l]), bf(xa_wo[l]), row(ln_g[l, 2]), row(ln_b[l, 2]), seq)

        h = _ffn(h, bf(ffn_wg[l, 1]), bf(ffn_wu[l, 1]), bf(ffn_wd[l, 1]), row(ln_g[l, 3]), row(ln_b[l, 3]))
    return h.reshape(batch, seq, d)
```

```python
import functools
import math

import jax
import jax.numpy as jnp
import numpy as np
from jax import lax
from jax.experimental import pallas as pl
from jax.experimental.pallas import tpu as pltpu

F32 = jnp.float32
BF16 = jnp.bfloat16

D_MODEL = 2048
DEPTH = 2
D_FF = 5632
MLA_HEADS = 8
Q_LORA = 512
KV_LORA = 512
NOPE_DIM = 128
ROPE_DIM = 64
MLA_V_DIM = 128
MLA_QK_PAD = 256
ROPE_THETA = 10000.0
SGU_CHUNK = 128
SGU_GROUPS = 4
SGU_GROUP_DIM = 128
SGU_WIDTH = SGU_GROUPS * SGU_GROUP_DIM
DIL_PATTERNS = ((128, 1), (512, 4), (2048, 16))
DIL_HEADS_PER_GROUP = 4
DIL_HEAD_DIM = 128
DIL_HEADS = DIL_HEADS_PER_GROUP * len(DIL_PATTERNS)
DIL_OUT = DIL_HEADS_PER_GROUP * DIL_HEAD_DIM
DIL_BLK = 128
REL_BUCKETS = 32
REL_MAX_DIST = 2048
XA_HEADS = 4
XA_HEAD_DIM = 128
XA_WIDTH = XA_HEADS * XA_HEAD_DIM
MLA_OUT = MLA_HEADS * MLA_V_DIM
ALPHA = (2 * DEPTH) ** 0.25
LN_EPS = 1e-5
RMS_EPS = 1e-6
NEG_INF = -1e30

VMEM_LIMIT_BYTES = 56 * 1024 * 1024
FFN_TF = 512
MERGE_TN = 512


def _params(*semantics):
    return pltpu.CompilerParams(dimension_semantics=semantics, vmem_limit_bytes=VMEM_LIMIT_BYTES)


def _ln_rows(y, g, b):
    mu = jnp.mean(y, axis=-1, keepdims=True)
    yc = y - mu
    var = jnp.mean(yc * yc, axis=-1, keepdims=True)
    return yc * lax.rsqrt(var + LN_EPS) * g + b


def _rms_rows(y, g):
    return y * lax.rsqrt(jnp.mean(y * y, axis=-1, keepdims=True) + RMS_EPS) * g


def _dot(a, b):
    return jnp.dot(a, b, preferred_element_type=F32)


def _dot_nt(a, b):
    return lax.dot_general(a, b, (((1,), (1,)), ((), ())), preferred_element_type=F32)


def _ffn_kernel(x_hbm, wgu_ref, wd_ref, gb_ref, o_ref, xbuf, xb_ref, sem, *, tm, tf):
    i = pl.program_id(0)
    f = pl.program_id(1)

    def x_copy(tile):
        return pltpu.make_async_copy(x_hbm.at[pl.ds(tile * tm, tm), :], xbuf, sem)

    @pl.when(f == 0)
    def _():
        @pl.when(i == 0)
        def _():
            x_copy(0).start()

        x_copy(i).wait()
        x = xbuf[...]
        xb_ref[...] = x.astype(BF16)
        o_ref[...] = ALPHA * x

        @pl.when(i + 1 < pl.num_programs(0))
        def _():
            x_copy(i + 1).start()

    gu = _dot(xb_ref[...], wgu_ref[0])
    gate = gu[:, :tf]
    h = (0.5 * (gate * jax.nn.sigmoid(gate)) * gu[:, tf:]).astype(BF16)
    o_ref[...] += _dot(h, wd_ref[...])

    @pl.when(f == pl.num_programs(1) - 1)
    def _():
        o_ref[...] = _ln_rows(o_ref[...], gb_ref[0:1, :], gb_ref[1:2, :])


def _ffn(x, wgu, wd, gb, tm=1024):
    t, d = x.shape
    nf, _, tf2 = wgu.shape
    tf = tf2 // 2
    return pl.pallas_call(
        functools.partial(_ffn_kernel, tm=tm, tf=tf),
        grid=(t // tm, nf),
        in_specs=[
            pl.BlockSpec(memory_space=pl.ANY),
            pl.BlockSpec((1, d, tf2), lambda i, f: (f, 0, 0)),
            pl.BlockSpec((tf, d), lambda i, f: (f, 0)),
            pl.BlockSpec((2, d), lambda i, f: (0, 0)),
        ],
        out_specs=pl.BlockSpec((tm, d), lambda i, f: (i, 0)),
        out_shape=jax.ShapeDtypeStruct((t, d), F32),
        scratch_shapes=[pltpu.VMEM((tm, d), F32), pltpu.VMEM((tm, d), BF16), pltpu.SemaphoreType.DMA(())],
        compiler_params=_params("arbitrary", "arbitrary"),
        name="ffn_swiglu_ln",
    )(x, wgu, wd, gb)


def _proj_kernel(x_ref, w_ref, o_ref):
    o_ref[...] = _dot(x_ref[...].astype(BF16), w_ref[...]).astype(o_ref.dtype)


def _proj(x, w, out_dtype, tm, tn):
    t, k = x.shape
    n = w.shape[1]
    tm = min(tm, t)
    assert t % tm == 0 and n % tn == 0
    return pl.pallas_call(
        _proj_kernel,
        grid=(t // tm, n // tn),
        in_specs=[
            pl.BlockSpec((tm, k), lambda i, j: (i, 0)),
            pl.BlockSpec((k, tn), lambda i, j: (0, j)),
        ],
        out_specs=pl.BlockSpec((tm, tn), lambda i, j: (i, j)),
        out_shape=jax.ShapeDtypeStruct((t, n), out_dtype),
        compiler_params=_params("parallel", "arbitrary"),
        name="proj",
    )(x, w)


def _mla_proj_kernel(x_ref, w1_ref, qn_ref, kvn_ref, wqn_ref, wqr_ref, wqrr_ref, wk_ref, wv_ref,
                     cos_ref, sin_ref, q_ref, k_ref, v_ref):
    xb = x_ref[...].astype(BF16)
    c = _dot(xb, w1_ref[...])
    cos = cos_ref[...]
    sin = sin_ref[...]
    cq = _rms_rows(c[:, :Q_LORA], qn_ref[...]).astype(BF16)
    ckv = _rms_rows(c[:, Q_LORA:Q_LORA + KV_LORA], kvn_ref[...]).astype(BF16)
    off = Q_LORA + KV_LORA
    k_rope = (c[:, off:off + 128] * cos + c[:, off + 128:off + 256] * sin).astype(BF16)
    q_nope = _dot(cq, wqn_ref[...])
    q_pe = _dot(cq, wqr_ref[...])
    q_pe_rot = _dot(cq, wqrr_ref[...])
    k_nope = _dot(ckv, wk_ref[...])
    v_ref[...] = _dot(ckv, wv_ref[...]).astype(BF16)
    for h in range(MLA_HEADS):
        lo = h * MLA_QK_PAD
        hs = slice(h * 128, (h + 1) * 128)
        q_ref[:, lo:lo + 128] = q_nope[:, hs].astype(BF16)
        q_ref[:, lo + 128:lo + 256] = (q_pe[:, hs] * cos + q_pe_rot[:, hs] * sin).astype(BF16)
        k_ref[:, lo:lo + 128] = k_nope[:, hs].astype(BF16)
        k_ref[:, lo + 128:lo + 256] = k_rope


def _mla_proj(x, w1, qn, kvn, wqn, wqr, wqrr, wk, wv, cos128, sin128, seq, tm=512):
    t, d = x.shape
    nseq = seq // tm
    full = lambda a: pl.BlockSpec(a.shape, lambda i: (0,) * a.ndim)
    hq = MLA_HEADS * MLA_QK_PAD
    return pl.pallas_call(
        _mla_proj_kernel,
        grid=(t // tm,),
        in_specs=[
            pl.BlockSpec((tm, d), lambda i: (i, 0)),
            full(w1), full(qn), full(kvn), full(wqn), full(wqr), full(wqrr), full(wk), full(wv),
            pl.BlockSpec((tm, 128), lambda i: (i % nseq, 0)),
            pl.BlockSpec((tm, 128), lambda i: (i % nseq, 0)),
        ],
        out_specs=[
            pl.BlockSpec((tm, hq), lambda i: (i, 0)),
            pl.BlockSpec((tm, hq), lambda i: (i, 0)),
            pl.BlockSpec((tm, MLA_OUT), lambda i: (i, 0)),
        ],
        out_shape=[
            jax.ShapeDtypeStruct((t, hq), BF16),
            jax.ShapeDtypeStruct((t, hq), BF16),
            jax.ShapeDtypeStruct((t, MLA_OUT), BF16),
        ],
        compiler_params=_params("parallel"),
        name="mla_proj",
    )(x, w1, qn, kvn, wqn, wqr, wqrr, wk, wv, cos128, sin128)


def _mla_attn_kernel(q_ref, k_ref, v_ref, o_ref, *, tq, scale):
    seq = q_ref.shape[1]
    for qi in range(seq // tq):
        kv_len = (qi + 1) * tq
        q = q_ref[0, qi * tq:(qi + 1) * tq, :]
        s = _dot_nt(q, k_ref[0, :kv_len, :]) * scale
        qpos = qi * tq + lax.broadcasted_iota(jnp.int32, (tq, kv_len), 0)
        kpos = lax.broadcasted_iota(jnp.int32, (tq, kv_len), 1)
        s = jnp.where(kpos <= qpos, s, NEG_INF)
        m = jnp.max(s, axis=-1, keepdims=True)
        e = jnp.exp(s - m)
        p = (e * (1.0 / jnp.sum(e, axis=-1, keepdims=True))).astype(BF16)
        o_ref[0, qi * tq:(qi + 1) * tq, :] = _dot(p, v_ref[0, :kv_len, :]).astype(o_ref.dtype)


def _mla_attn(q, k, v, tq=256):
    b, s, _ = q.shape
    kern = functools.partial(_mla_attn_kernel, tq=tq, scale=(NOPE_DIM + ROPE_DIM) ** -0.5)
    return pl.pallas_call(
        kern,
        grid=(b, MLA_HEADS),
        in_specs=[
            pl.BlockSpec((1, s, MLA_QK_PAD), lambda i, h: (i, 0, h)),
            pl.BlockSpec((1, s, MLA_QK_PAD), lambda i, h: (i, 0, h)),
            pl.BlockSpec((1, s, MLA_V_DIM), lambda i, h: (i, 0, h)),
        ],
        out_specs=pl.BlockSpec((1, s, MLA_V_DIM), lambda i, h: (i, 0, h)),
        out_shape=jax.ShapeDtypeStruct((b, s, MLA_OUT), BF16),
        compiler_params=_params("parallel", "parallel"),
        name="mla_attn",
    )(q, k, v)


def _sgu_kernel(x_ref, wz_ref, g_ref, b_ref, ws_ref, bs_ref, o_ref):
    tm = x_ref.shape[0]
    z = _dot(x_ref[...].astype(BF16), wz_ref[...])
    z = 0.5 * z * (1.0 + lax.erf(z * (2.0 ** -0.5)))
    u = z[:, :SGU_WIDTH]
    vn = _ln_rows(z[:, SGU_WIDTH:], g_ref[...], b_ref[...]).astype(BF16)
    row = lax.broadcasted_iota(jnp.int32, (SGU_CHUNK, SGU_CHUNK), 0)
    col = lax.broadcasted_iota(jnp.int32, (SGU_CHUNK, SGU_CHUNK), 1)
    for gi in range(SGU_GROUPS):
        w = jnp.where(col <= row, ws_ref[gi], 0.0).astype(BF16)
        bias = bs_ref[gi]
        cs = slice(gi * SGU_GROUP_DIM, (gi + 1) * SGU_GROUP_DIM)
        for ci in range(tm // SGU_CHUNK):
            rs = slice(ci * SGU_CHUNK, (ci + 1) * SGU_CHUNK)
            mixed = _dot(w, vn[rs, cs]) + bias
            o_ref[rs, cs] = (u[rs, cs] * mixed).astype(o_ref.dtype)


def _sgu(x, wz, g, b, ws, bs_b, tm=512):
    t, d = x.shape
    full = lambda a: pl.BlockSpec(a.shape, lambda i: (0,) * a.ndim)
    return pl.pallas_call(
        _sgu_kernel,
        grid=(t // tm,),
        in_specs=[pl.BlockSpec((tm, d), lambda i: (i, 0)), full(wz), full(g), full(b), full(ws), full(bs_b)],
        out_specs=pl.BlockSpec((tm, SGU_WIDTH), lambda i: (i, 0)),
        out_shape=jax.ShapeDtypeStruct((t, SGU_WIDTH), BF16),
        compiler_params=_params("parallel"),
        name="sgu",
    )(x, wz, g, b, ws, bs_b)


def _dil_attn_kernel(rel_ref, bucket_ref, q_ref, k_ref, v_ref, o_ref, l_ref, bias_ref, *, band, scale, group):
    sub = q_ref.shape[1]

    @pl.when((pl.program_id(0) == 0) & (pl.program_id(1) == 0))
    def _():
        bucket = bucket_ref[...]
        for h in range(DIL_HEADS_PER_GROUP):
            acc = jnp.zeros(bucket.shape, F32)
            for bk in range(REL_BUCKETS):
                acc = jnp.where(bucket == bk, rel_ref[bk, group * DIL_HEADS_PER_GROUP + h], acc)
            bias_ref[h] = acc

    qi = lax.broadcasted_iota(jnp.int32, (DIL_BLK, 2 * DIL_BLK), 0)
    ki = lax.broadcasted_iota(jnp.int32, (DIL_BLK, 2 * DIL_BLK), 1)
    dist = qi + DIL_BLK - ki
    in_band = (dist >= 0) & (dist <= band)
    for h in range(DIL_HEADS_PER_GROUP):
        cs = slice(h * DIL_HEAD_DIM, (h + 1) * DIL_HEAD_DIM)
        bias = bias_ref[h]
        for n in range(sub // DIL_BLK):
            rs = slice(n * DIL_BLK, (n + 1) * DIL_BLK)
            q = q_ref[0, rs, cs]
            if n == 0:
                s = _dot_nt(q, k_ref[0, rs, cs]) * scale + bias[:, DIL_BLK:]
                s = jnp.where(in_band[:, DIL_BLK:], s, NEG_INF)
                vv = v_ref[0, rs, cs]
            else:
                ks = slice((n - 1) * DIL_BLK, (n + 1) * DIL_BLK)
                s = _dot_nt(q, k_ref[0, ks, cs]) * scale + bias
                s = jnp.where(in_band, s, NEG_INF)
                vv = v_ref[0, ks, cs]
            m = jnp.max(s, axis=-1, keepdims=True)
            e = jnp.exp(s - m)
            den = jnp.sum(e, axis=-1, keepdims=True)
            p = (e * (1.0 / den)).astype(BF16)
            o_ref[0, rs, cs] = _dot(p, vv)
            l_ref[0, rs, cs] = jnp.broadcast_to(m + jnp.log(den), (DIL_BLK, DIL_HEAD_DIM))


def _dil_attn(qkv, rel_bias, bucket, group, dil, band):
    b, sub, _ = qkv.shape
    ngroups = len(DIL_PATTERNS)
    kern = functools.partial(_dil_attn_kernel, band=band, scale=DIL_HEAD_DIM ** -0.5, group=group)

    def in_blk(c):
        return pl.BlockSpec((1, sub, DIL_OUT), lambda i, r: (i, 0, r * 3 * ngroups + c * ngroups + group))

    out_blk = pl.BlockSpec((1, sub, DIL_OUT), lambda i, r: (i, 0, r))
    return pl.pallas_call(
        kern,
        grid=(b, dil),
        in_specs=[pl.BlockSpec(memory_space=pltpu.SMEM),
                  pl.BlockSpec(bucket.shape, lambda i, r: (0, 0)),
                  in_blk(0), in_blk(1), in_blk(2)],
        out_specs=[out_blk, out_blk],
        out_shape=[jax.ShapeDtypeStruct((b, sub, dil * DIL_OUT), F32)] * 2,
        scratch_shapes=[pltpu.VMEM((DIL_HEADS_PER_GROUP, DIL_BLK, 2 * DIL_BLK), F32)],
        compiler_params=_params("arbitrary", "arbitrary"),
        name="dil_attn",
    )(rel_bias, bucket, qkv, qkv, qkv)


def _dil_combine_kernel(o0, o1, o2, l0, l1, l2, out_ref):
    a0, a1, a2 = l0[...], l1[...], l2[...]
    m = jnp.maximum(jnp.maximum(a0, a1), a2)
    e0, e1, e2 = jnp.exp(a0 - m), jnp.exp(a1 - m), jnp.exp(a2 - m)
    inv = 1.0 / (e0 + e1 + e2)
    out_ref[...] = ((e0 * inv) * o0[...] + (e1 * inv) * o1[...] + (e2 * inv) * o2[...]).astype(out_ref.dtype)


def _dil_combine(outs, lses, tm=1024):
    t, w = outs[0].shape
    blk = pl.BlockSpec((tm, w), lambda i: (i, 0))
    return pl.pallas_call(
        _dil_combine_kernel,
        grid=(t // tm,),
        in_specs=[blk] * 6,
        out_specs=blk,
        out_shape=jax.ShapeDtypeStruct((t, w), BF16),
        compiler_params=_params("parallel"),
        name="dil_combine",
    )(*outs, *lses)


def _merge_kernel(x_ref, oa_ref, ob_ref, oc_ref, wg_ref, p_ref, wo_ref, gb_ref, o_ref, xb_ref, *, tn):
    j = pl.program_id(1)

    @pl.when(j == 0)
    def _():
        x = x_ref[...]
        xb_ref[...] = x.astype(BF16)
        o_ref[...] = ALPHA * x

    gates = jax.nn.sigmoid(_dot(xb_ref[...], wg_ref[0]))
    o_sgu = MLA_OUT + SGU_WIDTH
    merged = gates[:, :tn] * _dot(oa_ref[...], p_ref[:MLA_OUT, :])
    merged += gates[:, tn:2 * tn] * _dot(ob_ref[...], p_ref[MLA_OUT:o_sgu, :])
    merged += gates[:, 2 * tn:] * _dot(oc_ref[...], p_ref[o_sgu:, :])
    o_ref[...] += _dot(merged.astype(BF16), wo_ref[...])

    @pl.when(j == pl.num_programs(1) - 1)
    def _():
        o_ref[...] = _ln_rows(o_ref[...], gb_ref[0:1, :], gb_ref[1:2, :])


def _merge(x, oa, ob, oc, wg3, p, wo, gb, tm=512):
    t, d = x.shape
    nj, _, tn3 = wg3.shape
    tn = tn3 // 3
    rows = lambda a: pl.BlockSpec((tm, a.shape[1]), lambda i, j: (i, 0))
    return pl.pallas_call(
        functools.partial(_merge_kernel, tn=tn),
        grid=(t // tm, nj),
        in_specs=[rows(x), rows(oa), rows(ob), rows(oc),
                  pl.BlockSpec((1, d, tn3), lambda i, j: (j, 0, 0)),
                  pl.BlockSpec((p.shape[0], tn), lambda i, j: (0, j)),
                  pl.BlockSpec((tn, d), lambda i, j: (j, 0)),
                  pl.BlockSpec((2, d), lambda i, j: (0, 0))],
        out_specs=pl.BlockSpec((tm, d), lambda i, j: (i, 0)),
        out_shape=jax.ShapeDtypeStruct((t, d), F32),
        scratch_shapes=[pltpu.VMEM((tm, d), BF16)],
        compiler_params=_params("parallel", "arbitrary"),
        name="merge_out_ln",
    )(x, oa, ob, oc, wg3, p, wo, gb)


def _xa_kernel(x_ref, kv_ref, wq_ref, wo_ref, g_ref, b_ref, o_ref):
    x = x_ref[...]
    q = _dot(x.astype(BF16), wq_ref[...]).astype(BF16)
    heads = []
    for h in range(XA_HEADS):
        cs = slice(h * XA_HEAD_DIM, (h + 1) * XA_HEAD_DIM)
        s = _dot_nt(q[:, cs], kv_ref[0, :, cs]) * (XA_HEAD_DIM ** -0.5)
        m = jnp.max(s, axis=-1, keepdims=True)
        e = jnp.exp(s - m)
        p = (e * (1.0 / jnp.sum(e, axis=-1, keepdims=True))).astype(BF16)
        heads.append(_dot(p, kv_ref[0, :, XA_WIDTH + h * XA_HEAD_DIM:XA_WIDTH + (h + 1) * XA_HEAD_DIM]))
    o = jnp.concatenate(heads, axis=-1).astype(BF16)
    o_ref[...] = _ln_rows(ALPHA * x + _dot(o, wo_ref[...]), g_ref[...], b_ref[...])


def _xa(x, kv, wq, wo, g, b, seq, tm=512):
    t, d = x.shape
    nseq = seq // tm
    full = lambda a: pl.BlockSpec(a.shape, lambda i: (0,) * a.ndim)
    return pl.pallas_call(
        _xa_kernel,
        grid=(t // tm,),
        in_specs=[pl.BlockSpec((tm, d), lambda i: (i, 0)),
                  pl.BlockSpec((1,) + kv.shape[1:], lambda i: (i // nseq, 0, 0)),
                  full(wq), full(wo), full(g), full(b)],
        out_specs=pl.BlockSpec((tm, d), lambda i: (i, 0)),
        out_shape=jax.ShapeDtypeStruct((t, d), F32),
        compiler_params=_params("parallel"),
        name="xattn_ln",
    )(x, kv, wq, wo, g, b)


def _rope_tables128(seq):
    inv = ROPE_THETA ** (-jnp.arange(0, ROPE_DIM, 2, dtype=F32) / ROPE_DIM)
    ang = jnp.arange(seq, dtype=F32)[:, None] * inv[None, :]
    z = jnp.zeros((seq, 128 - ROPE_DIM), F32)
    cos, sin = jnp.cos(ang), jnp.sin(ang)
    return jnp.concatenate([cos, cos, z], axis=1), jnp.concatenate([sin, sin, z], axis=1)


def _rot_cols(w):
    half = w.shape[-1] // 2
    return jnp.concatenate([-w[..., half:], w[..., :half]], axis=-1)


def _pad_last(w, width):
    return jnp.pad(w, [(0, 0)] * (w.ndim - 1) + [(0, width - w.shape[-1])])


def _t5_bucket(dist):
    exact = REL_BUCKETS // 2
    df = jnp.maximum(dist, 1).astype(F32)
    large = exact + (jnp.log(df / exact) / math.log(REL_MAX_DIST / exact) * (REL_BUCKETS - exact)).astype(jnp.int32)
    large = jnp.minimum(large, REL_BUCKETS - 1)
    return jnp.where(dist < exact, dist, large)


def _dil_buckets(dil):
    qi = jnp.arange(DIL_BLK)[:, None]
    ki = jnp.arange(2 * DIL_BLK)[None, :]
    return _t5_bucket(jnp.maximum(qi + DIL_BLK - ki, 0) * dil).astype(jnp.int32)


def _col_tiles(w, tn):
    k, n = w.shape
    return w.reshape(k, n // tn, tn).transpose(1, 0, 2)


def kernel(x, mem, ln_g, ln_b, ffn_wg, ffn_wu, ffn_wd, w_in, mla_q_norm, mla_kv_norm, mla_w_uq, mla_w_ukv,
           sgu_ln_g, sgu_ln_b, sgu_ws, sgu_bs, rel_bias, w_branch, w_out, xa_wq, xa_wkv, xa_wo):
    batch, seq, d = x.shape
    t = batch * seq
    mem_len = mem.shape[1]
    cos128, sin128 = _rope_tables128(seq)
    h = x.reshape(t, d)
    mem2 = mem.reshape(batch * mem_len, d)
    bf = lambda a: a.astype(BF16)
    row = lambda a: a.reshape(1, -1)

    o_cq = 0
    o_ckv = o_cq + Q_LORA
    o_kpe = o_ckv + KV_LORA
    o_z = o_kpe + ROPE_DIM
    o_qkv = o_z + 2 * SGU_WIDTH
    o_gate = o_qkv + 3 * DIL_HEADS * DIL_HEAD_DIM

    def ffn(h, l, i, k):
        wgu = jnp.concatenate([_col_tiles(ffn_wg[l, i], FFN_TF), _col_tiles(ffn_wu[l, i], FFN_TF)], axis=2)
        return _ffn(h, bf(wgu), bf(ffn_wd[l, i]), jnp.stack([ln_g[l, k], ln_b[l, k]]))

    buckets = [_dil_buckets(dil) for _, dil in DIL_PATTERNS]

    for l in range(DEPTH):
        h = ffn(h, l, 0, 0)

        wl = w_in[l]
        w_kpe = wl[:, o_kpe:o_z]
        w1 = bf(jnp.concatenate([wl[:, o_cq:o_kpe], _pad_last(w_kpe, 128), _pad_last(_rot_cols(w_kpe), 128)], axis=1))
        wuq = mla_w_uq[l].reshape(Q_LORA, MLA_HEADS, NOPE_DIM + ROPE_DIM)
        wqn = bf(wuq[:, :, :NOPE_DIM].reshape(Q_LORA, -1))
        wqr = bf(_pad_last(wuq[:, :, NOPE_DIM:], 128).reshape(Q_LORA, -1))
        wqrr = bf(_pad_last(_rot_cols(wuq[:, :, NOPE_DIM:]), 128).reshape(Q_LORA, -1))
        wukv = mla_w_ukv[l].reshape(KV_LORA, MLA_HEADS, NOPE_DIM + MLA_V_DIM)
        wk = bf(wukv[:, :, :NOPE_DIM].reshape(KV_LORA, -1))
        wv = bf(wukv[:, :, NOPE_DIM:].reshape(KV_LORA, -1))
        q, k, v = _mla_proj(h, w1, row(mla_q_norm[l]), row(mla_kv_norm[l]), wqn, wqr, wqrr, wk, wv, cos128, sin128, seq)
        hq = MLA_HEADS * MLA_QK_PAD
        o_a = _mla_attn(q.reshape(batch, seq, hq), k.reshape(batch, seq, hq), v.reshape(batch, seq, MLA_OUT))
        o_a = o_a.reshape(t, MLA_OUT)

        bs_b = jnp.broadcast_to(sgu_bs[l][:, :, None], (SGU_GROUPS, SGU_CHUNK, SGU_GROUP_DIM))
        o_b = _sgu(h, bf(wl[:, o_z:o_qkv]), row(sgu_ln_g[l]), row(sgu_ln_b[l]), sgu_ws[l], bs_b)

        qkv = _proj(h, bf(wl[:, o_qkv:o_gate]), BF16, tm=1024, tn=DIL_HEADS * DIL_HEAD_DIM)
        outs, lses = [], []
        for gi, (window, dil) in enumerate(DIL_PATTERNS):
            o_g, l_g = _dil_attn(qkv.reshape(batch, seq // dil, dil * qkv.shape[1]), rel_bias, buckets[gi],
                                 gi, dil, window // dil)
            outs.append(o_g.reshape(t, DIL_OUT))
            lses.append(l_g.reshape(t, DIL_OUT))
        o_c = _dil_combine(outs, lses)

        wg = wl[:, o_gate:]
        wg3 = jnp.concatenate([_col_tiles(wg[:, c * d:(c + 1) * d], MERGE_TN) for c in range(3)], axis=2)
        h = _merge(h, o_a, o_b, o_c, bf(wg3), bf(w_branch[l]), bf(w_out[l]), jnp.stack([ln_g[l, 1], ln_b[l, 1]]))

        kv = _proj(mem2, bf(xa_wkv[l]), BF16, tm=1024, tn=2 * XA_WIDTH).reshape(batch, mem_len, 2 * XA_WIDTH)
        h = _xa(h, kv, bf(xa_wq[l]), bf(xa_wo[l]), row(ln_g[l, 2]), row(ln_b[l, 2]), seq)

        h = ffn(h, l, 1, 3)
    return h.reshape(batch, seq, d)
```

```python
import functools
import math

import jax
import jax.numpy as jnp
import numpy as np
from jax import lax
from jax.experimental import pallas as pl
from jax.experimental.pallas import tpu as pltpu

F32 = jnp.float32
BF16 = jnp.bfloat16

D_MODEL = 2048
DEPTH = 2
D_FF = 5632
MLA_HEADS = 8
Q_LORA = 512
KV_LORA = 512
NOPE_DIM = 128
ROPE_DIM = 64
MLA_V_DIM = 128
MLA_QK_PAD = 256
ROPE_THETA = 10000.0
SGU_CHUNK = 128
SGU_GROUPS = 4
SGU_GROUP_DIM = 128
SGU_WIDTH = SGU_GROUPS * SGU_GROUP_DIM
DIL_PATTERNS = ((128, 1), (512, 4), (2048, 16))
DIL_HEADS_PER_GROUP = 4
DIL_HEAD_DIM = 128
DIL_HEADS = DIL_HEADS_PER_GROUP * len(DIL_PATTERNS)
DIL_OUT = DIL_HEADS_PER_GROUP * DIL_HEAD_DIM
DIL_BLK = 128
REL_BUCKETS = 32
REL_MAX_DIST = 2048
XA_HEADS = 4
XA_HEAD_DIM = 128
XA_WIDTH = XA_HEADS * XA_HEAD_DIM
MLA_OUT = MLA_HEADS * MLA_V_DIM
ALPHA = (2 * DEPTH) ** 0.25
LN_EPS = 1e-5
RMS_EPS = 1e-6
NEG_INF = -1e30

VMEM_LIMIT_BYTES = 56 * 1024 * 1024
FFN_TF = 512
MERGE_TN = 512


def _params(*semantics):
    return pltpu.CompilerParams(dimension_semantics=semantics, vmem_limit_bytes=VMEM_LIMIT_BYTES)


def _ln_rows(y, g, b):
    mu = jnp.mean(y, axis=-1, keepdims=True)
    yc = y - mu
    var = jnp.mean(yc * yc, axis=-1, keepdims=True)
    return yc * lax.rsqrt(var + LN_EPS) * g + b


def _rms_rows(y, g):
    return y * lax.rsqrt(jnp.mean(y * y, axis=-1, keepdims=True) + RMS_EPS) * g


def _dot(a, b):
    return jnp.dot(a, b, preferred_element_type=F32)


def _dot_nt(a, b):
    return lax.dot_general(a, b, (((1,), (1,)), ((), ())), preferred_element_type=F32)


def _ffn_kernel(x_hbm, wg_ref, wu_ref, wd_ref, gb_ref, o_ref, xbuf, xb_ref, sem, *, tm):
    i = pl.program_id(0)
    f = pl.program_id(1)

    def x_copy(tile):
        return pltpu.make_async_copy(x_hbm.at[pl.ds(tile * tm, tm), :], xbuf, sem)

    @pl.when(f == 0)
    def _():
        @pl.when(i == 0)
        def _():
            x_copy(0).start()

        x_copy(i).wait()
        x = xbuf[...]
        xb_ref[...] = x.astype(BF16)
        o_ref[...] = ALPHA * x

        @pl.when(i + 1 < pl.num_programs(0))
        def _():
            x_copy(i + 1).start()

    xb = xb_ref[...]
    gate = _dot(xb, wg_ref[...])
    up = _dot(xb, wu_ref[...])
    h = (0.5 * (gate * jax.nn.sigmoid(gate)) * up).astype(BF16)
    o_ref[...] += _dot(h, wd_ref[...])

    @pl.when(f == pl.num_programs(1) - 1)
    def _():
        o_ref[...] = _ln_rows(o_ref[...], gb_ref[0:1, :], gb_ref[1:2, :])


def _ffn(x, wg, wu, wd, gb, tm=1024, tf=FFN_TF):
    t, d = x.shape
    return pl.pallas_call(
        functools.partial(_ffn_kernel, tm=tm),
        grid=(t // tm, wg.shape[1] // tf),
        in_specs=[
            pl.BlockSpec(memory_space=pl.ANY),
            pl.BlockSpec((d, tf), lambda i, f: (0, f)),
            pl.BlockSpec((d, tf), lambda i, f: (0, f)),
            pl.BlockSpec((tf, d), lambda i, f: (f, 0)),
            pl.BlockSpec((2, d), lambda i, f: (0, 0)),
        ],
        out_specs=pl.BlockSpec((tm, d), lambda i, f: (i, 0)),
        out_shape=jax.ShapeDtypeStruct((t, d), F32),
        scratch_shapes=[pltpu.VMEM((tm, d), F32), pltpu.VMEM((tm, d), BF16), pltpu.SemaphoreType.DMA(())],
        compiler_params=_params("arbitrary", "arbitrary"),
        name="ffn_swiglu_ln",
    )(x, wg, wu, wd, gb)


def _proj_kernel(x_ref, w_ref, o_ref):
    o_ref[...] = _dot(x_ref[...].astype(BF16), w_ref[...]).astype(o_ref.dtype)


def _proj(x, w, out_dtype, tm, tn):
    t, k = x.shape
    n = w.shape[1]
    tm = min(tm, t)
    assert t % tm == 0 and n % tn == 0
    return pl.pallas_call(
        _proj_kernel,
        grid=(t // tm, n // tn),
        in_specs=[
            pl.BlockSpec((tm, k), lambda i, j: (i, 0)),
            pl.BlockSpec((k, tn), lambda i, j: (0, j)),
        ],
        out_specs=pl.BlockSpec((tm, tn), lambda i, j: (i, j)),
        out_shape=jax.ShapeDtypeStruct((t, n), out_dtype),
        compiler_params=_params("parallel", "arbitrary"),
        name="proj",
    )(x, w)


def _mla_proj_kernel(x_ref, w1_ref, qn_ref, kvn_ref, wqn_ref, wqr_ref, wqrr_ref, wk_ref, wv_ref,
                     cos_ref, sin_ref, q_ref, k_ref, v_ref):
    xb = x_ref[...].astype(BF16)
    c = _dot(xb, w1_ref[...])
    cos = cos_ref[...]
    sin = sin_ref[...]
    cq = _rms_rows(c[:, :Q_LORA], qn_ref[...]).astype(BF16)
    ckv = _rms_rows(c[:, Q_LORA:Q_LORA + KV_LORA], kvn_ref[...]).astype(BF16)
    off = Q_LORA + KV_LORA
    k_rope = (c[:, off:off + 128] * cos + c[:, off + 128:off + 256] * sin).astype(BF16)
    q_nope = _dot(cq, wqn_ref[...])
    q_pe = _dot(cq, wqr_ref[...])
    q_pe_rot = _dot(cq, wqrr_ref[...])
    k_nope = _dot(ckv, wk_ref[...])
    v_ref[...] = _dot(ckv, wv_ref[...]).astype(BF16)
    for h in range(MLA_HEADS):
        lo = h * MLA_QK_PAD
        hs = slice(h * 128, (h + 1) * 128)
        q_ref[:, lo:lo + 128] = q_nope[:, hs].astype(BF16)
        q_ref[:, lo + 128:lo + 256] = (q_pe[:, hs] * cos + q_pe_rot[:, hs] * sin).astype(BF16)
        k_ref[:, lo:lo + 128] = k_nope[:, hs].astype(BF16)
        k_ref[:, lo + 128:lo + 256] = k_rope


def _mla_proj(x, w1, qn, kvn, wqn, wqr, wqrr, wk, wv, cos128, sin128, seq, tm=512):
    t, d = x.shape
    nseq = seq // tm
    full = lambda a: pl.BlockSpec(a.shape, lambda i: (0,) * a.ndim)
    hq = MLA_HEADS * MLA_QK_PAD
    return pl.pallas_call(
        _mla_proj_kernel,
        grid=(t // tm,),
        in_specs=[
            pl.BlockSpec((tm, d), lambda i: (i, 0)),
            full(w1), full(qn), full(kvn), full(wqn), full(wqr), full(wqrr), full(wk), full(wv),
            pl.BlockSpec((tm, 128), lambda i: (i % nseq, 0)),
            pl.BlockSpec((tm, 128), lambda i: (i % nseq, 0)),
        ],
        out_specs=[
            pl.BlockSpec((tm, hq), lambda i: (i, 0)),
            pl.BlockSpec((tm, hq), lambda i: (i, 0)),
            pl.BlockSpec((tm, MLA_OUT), lambda i: (i, 0)),
        ],
        out_shape=[
            jax.ShapeDtypeStruct((t, hq), BF16),
            jax.ShapeDtypeStruct((t, hq), BF16),
            jax.ShapeDtypeStruct((t, MLA_OUT), BF16),
        ],
        compiler_params=_params("parallel"),
        name="mla_proj",
    )(x, w1, qn, kvn, wqn, wqr, wqrr, wk, wv, cos128, sin128)


def _mla_attn_kernel(q_ref, k_ref, v_ref, o_ref, *, tq, scale):
    seq = q_ref.shape[1]
    c = scale * math.log2(math.e)
    causal = (lax.broadcasted_iota(jnp.int32, (tq, tq), 1) <= lax.broadcasted_iota(jnp.int32, (tq, tq), 0))
    for qi in range(seq // tq):
        lo = qi * tq
        rows = slice(lo, lo + tq)
        q = q_ref[0, rows, :]
        s_d = jnp.where(causal, _dot_nt(q, k_ref[0, rows, :]), NEG_INF)
        m = jnp.max(s_d, axis=-1, keepdims=True)
        if qi > 0:
            s_p = _dot_nt(q, k_ref[0, :lo, :])
            m = jnp.maximum(m, jnp.max(s_p, axis=-1, keepdims=True))
            e_p = jnp.exp2((s_p - m) * c)
        e_d = jnp.exp2((s_d - m) * c)
        den = jnp.sum(e_d, axis=-1, keepdims=True)
        if qi > 0:
            den = den + jnp.sum(e_p, axis=-1, keepdims=True)
        inv = 1.0 / den
        o = _dot((e_d * inv).astype(BF16), v_ref[0, rows, :])
        if qi > 0:
            o = o + _dot((e_p * inv).astype(BF16), v_ref[0, :lo, :])
        o_ref[0, rows, :] = o.astype(o_ref.dtype)


def _mla_attn(q, k, v, tq=256):
    b, s, _ = q.shape
    kern = functools.partial(_mla_attn_kernel, tq=tq, scale=(NOPE_DIM + ROPE_DIM) ** -0.5)
    return pl.pallas_call(
        kern,
        grid=(b, MLA_HEADS),
        in_specs=[
            pl.BlockSpec((1, s, MLA_QK_PAD), lambda i, h: (i, 0, h)),
            pl.BlockSpec((1, s, MLA_QK_PAD), lambda i, h: (i, 0, h)),
            pl.BlockSpec((1, s, MLA_V_DIM), lambda i, h: (i, 0, h)),
        ],
        out_specs=pl.BlockSpec((1, s, MLA_V_DIM), lambda i, h: (i, 0, h)),
        out_shape=jax.ShapeDtypeStruct((b, s, MLA_OUT), BF16),
        compiler_params=_params("parallel", "parallel"),
        name="mla_attn",
    )(q, k, v)


def _sgu_kernel(x_ref, wz_ref, g_ref, b_ref, ws_ref, bs_ref, o_ref):
    tm = x_ref.shape[0]
    z = _dot(x_ref[...].astype(BF16), wz_ref[...])
    z = 0.5 * z * (1.0 + lax.erf(z * (2.0 ** -0.5)))
    u = z[:, :SGU_WIDTH]
    vn = _ln_rows(z[:, SGU_WIDTH:], g_ref[...], b_ref[...]).astype(BF16)
    row = lax.broadcasted_iota(jnp.int32, (SGU_CHUNK, SGU_CHUNK), 0)
    col = lax.broadcasted_iota(jnp.int32, (SGU_CHUNK, SGU_CHUNK), 1)
    for gi in range(SGU_GROUPS):
        w = jnp.where(col <= row, ws_ref[gi], 0.0).astype(BF16)
        bias = bs_ref[gi]
        cs = slice(gi * SGU_GROUP_DIM, (gi + 1) * SGU_GROUP_DIM)
        for ci in range(tm // SGU_CHUNK):
            rs = slice(ci * SGU_CHUNK, (ci + 1) * SGU_CHUNK)
            mixed = _dot(w, vn[rs, cs]) + bias
            o_ref[rs, cs] = (u[rs, cs] * mixed).astype(o_ref.dtype)


def _sgu(x, wz, g, b, ws, bs_b, tm=512):
    t, d = x.shape
    full = lambda a: pl.BlockSpec(a.shape, lambda i: (0,) * a.ndim)
    return pl.pallas_call(
        _sgu_kernel,
        grid=(t // tm,),
        in_specs=[pl.BlockSpec((tm, d), lambda i: (i, 0)), full(wz), full(g), full(b), full(ws), full(bs_b)],
        out_specs=pl.BlockSpec((tm, SGU_WIDTH), lambda i: (i, 0)),
        out_shape=jax.ShapeDtypeStruct((t, SGU_WIDTH), BF16),
        compiler_params=_params("parallel"),
        name="sgu",
    )(x, wz, g, b, ws, bs_b)


def _dil_attn_kernel(rel_ref, bucket_ref, q_ref, k_ref, v_ref, o_ref, l_ref, bias_ref, *stage, dil, band, scale, group):
    sub = q_ref.shape[1] // dil

    @pl.when(pl.program_id(0) == 0)
    def _():
        bucket = bucket_ref[...]
        for h in range(DIL_HEADS_PER_GROUP):
            acc = jnp.zeros(bucket.shape, F32)
            for bk in range(REL_BUCKETS):
                acc = jnp.where(bucket == bk, rel_ref[bk, group * DIL_HEADS_PER_GROUP + h], acc)
            bias_ref[h] = acc

    if dil > 1:
        for src, dst in zip((q_ref, k_ref, v_ref), stage):
            for h in range(DIL_HEADS_PER_GROUP):
                dst[h] = src[0, :, h * DIL_HEAD_DIM:(h + 1) * DIL_HEAD_DIM].astype(F32)

    def rows(r, first_blk, nblk):
        if dil == 1:
            return pl.ds(first_blk * DIL_BLK, nblk * DIL_BLK)
        return pl.ds(r + first_blk * DIL_BLK * dil, nblk * DIL_BLK, stride=dil)

    def load(idx, h, rs):
        if dil == 1:
            return (q_ref, k_ref, v_ref)[idx][0, rs, h * DIL_HEAD_DIM:(h + 1) * DIL_HEAD_DIM]
        return stage[idx][h, rs, :].astype(BF16)

    qi = lax.broadcasted_iota(jnp.int32, (DIL_BLK, 2 * DIL_BLK), 0)
    ki = lax.broadcasted_iota(jnp.int32, (DIL_BLK, 2 * DIL_BLK), 1)
    dist = qi + DIL_BLK - ki
    in_band = (dist >= 0) & (dist <= band)
    for h in range(DIL_HEADS_PER_GROUP):
        bias = bias_ref[h]
        for r in range(dil):
            for n in range(sub // DIL_BLK):
                rs = rows(r, n, 1)
                q = load(0, h, rs)
                if n == 0:
                    s = _dot_nt(q, load(1, h, rs)) * scale + bias[:, DIL_BLK:]
                    s = jnp.where(in_band[:, DIL_BLK:], s, NEG_INF)
                    vv = load(2, h, rs)
                else:
                    ks = rows(r, n - 1, 2)
                    s = _dot_nt(q, load(1, h, ks)) * scale + bias
                    s = jnp.where(in_band, s, NEG_INF)
                    vv = load(2, h, ks)
                m = jnp.max(s, axis=-1, keepdims=True)
                e = jnp.exp(s - m)
                den = jnp.sum(e, axis=-1, keepdims=True)
                p = (e * (1.0 / den)).astype(BF16)
                o_ref[0, h, rs, :] = _dot(p, vv)
                l_ref[0, h, rs, :] = jnp.broadcast_to(m + jnp.log(den), (DIL_BLK, DIL_HEAD_DIM))


def _dil_attn(qkv, rel_bias, bucket, group, dil, band):
    b, seq, _ = qkv.shape
    ngroups = len(DIL_PATTERNS)
    kern = functools.partial(_dil_attn_kernel, dil=dil, band=band, scale=DIL_HEAD_DIM ** -0.5, group=group)

    def in_blk(c):
        return pl.BlockSpec((1, seq, DIL_OUT), lambda i: (i, 0, c * ngroups + group))

    out_shape = (b, DIL_HEADS_PER_GROUP, seq, DIL_HEAD_DIM)
    out_blk = pl.BlockSpec((1,) + out_shape[1:], lambda i: (i, 0, 0, 0))
    stage = [pltpu.VMEM(out_shape[1:], F32)] * 3 if dil > 1 else []
    return pl.pallas_call(
        kern,
        grid=(b,),
        in_specs=[pl.BlockSpec(memory_space=pltpu.SMEM),
                  pl.BlockSpec(bucket.shape, lambda i: (0, 0)),
                  in_blk(0), in_blk(1), in_blk(2)],
        out_specs=[out_blk, out_blk],
        out_shape=[jax.ShapeDtypeStruct(out_shape, F32)] * 2,
        scratch_shapes=[pltpu.VMEM((DIL_HEADS_PER_GROUP, DIL_BLK, 2 * DIL_BLK), F32)] + stage,
        compiler_params=_params("arbitrary"),
        name="dil_attn",
    )(rel_bias, bucket, qkv, qkv, qkv)


def _dil_combine_kernel(o0, o1, o2, l0, l1, l2, out_ref):
    for h in range(DIL_HEADS_PER_GROUP):
        a0, a1, a2 = l0[0, h], l1[0, h], l2[0, h]
        m = jnp.maximum(jnp.maximum(a0, a1), a2)
        e0, e1, e2 = jnp.exp(a0 - m), jnp.exp(a1 - m), jnp.exp(a2 - m)
        inv = 1.0 / (e0 + e1 + e2)
        o = (e0 * inv) * o0[0, h] + (e1 * inv) * o1[0, h] + (e2 * inv) * o2[0, h]
        out_ref[:, h * DIL_HEAD_DIM:(h + 1) * DIL_HEAD_DIM] = o.astype(out_ref.dtype)


def _dil_combine(outs, lses, ts=1024):
    b, nh, seq, hd = outs[0].shape
    nseq = seq // ts
    blk = pl.BlockSpec((1, nh, ts, hd), lambda i, j: (i, 0, j, 0))
    return pl.pallas_call(
        _dil_combine_kernel,
        grid=(b, nseq),
        in_specs=[blk] * 6,
        out_specs=pl.BlockSpec((ts, nh * hd), lambda i, j: (i * nseq + j, 0)),
        out_shape=jax.ShapeDtypeStruct((b * seq, nh * hd), BF16),
        compiler_params=_params("parallel", "parallel"),
        name="dil_combine",
    )(*outs, *lses)


def _merge_kernel(x_ref, oa_ref, ob_ref, oc_ref, wga_ref, wgb_ref, wgc_ref, p_ref, wo_ref, gb_ref, o_ref, xb_ref):
    j = pl.program_id(1)

    @pl.when(j == 0)
    def _():
        x = x_ref[...]
        xb_ref[...] = x.astype(BF16)
        o_ref[...] = ALPHA * x

    xb = xb_ref[...]
    o_sgu = MLA_OUT + SGU_WIDTH
    merged = jax.nn.sigmoid(_dot(xb, wga_ref[...])) * _dot(oa_ref[...], p_ref[:MLA_OUT, :])
    merged += jax.nn.sigmoid(_dot(xb, wgb_ref[...])) * _dot(ob_ref[...], p_ref[MLA_OUT:o_sgu, :])
    merged += jax.nn.sigmoid(_dot(xb, wgc_ref[...])) * _dot(oc_ref[...], p_ref[o_sgu:, :])
    o_ref[...] += _dot(merged.astype(BF16), wo_ref[...])

    @pl.when(j == pl.num_programs(1) - 1)
    def _():
        o_ref[...] = _ln_rows(o_ref[...], gb_ref[0:1, :], gb_ref[1:2, :])


def _merge(x, oa, ob, oc, wg, p, wo, gb, tm=512, tn=MERGE_TN):
    t, d = x.shape
    nj = d // tn
    rows = lambda a: pl.BlockSpec((tm, a.shape[1]), lambda i, j: (i, 0))
    gate = lambda c: pl.BlockSpec((d, tn), lambda i, j: (0, c * nj + j))
    return pl.pallas_call(
        _merge_kernel,
        grid=(t // tm, nj),
        in_specs=[rows(x), rows(oa), rows(ob), rows(oc), gate(0), gate(1), gate(2),
                  pl.BlockSpec((p.shape[0], tn), lambda i, j: (0, j)),
                  pl.BlockSpec((tn, d), lambda i, j: (j, 0)),
                  pl.BlockSpec((2, d), lambda i, j: (0, 0))],
        out_specs=pl.BlockSpec((tm, d), lambda i, j: (i, 0)),
        out_shape=jax.ShapeDtypeStruct((t, d), F32),
        scratch_shapes=[pltpu.VMEM((tm, d), BF16)],
        compiler_params=_params("parallel", "arbitrary"),
        name="merge_out_ln",
    )(x, oa, ob, oc, wg, wg, wg, p, wo, gb)


def _xa_kernel(x_ref, kv_ref, wq_ref, wo_ref, g_ref, b_ref, o_ref):
    x = x_ref[...]
    q = _dot(x.astype(BF16), wq_ref[...]).astype(BF16)
    heads = []
    for h in range(XA_HEADS):
        cs = slice(h * XA_HEAD_DIM, (h + 1) * XA_HEAD_DIM)
        s = _dot_nt(q[:, cs], kv_ref[0, :, cs]) * (XA_HEAD_DIM ** -0.5)
        m = jnp.max(s, axis=-1, keepdims=True)
        e = jnp.exp(s - m)
        p = (e * (1.0 / jnp.sum(e, axis=-1, keepdims=True))).astype(BF16)
        heads.append(_dot(p, kv_ref[0, :, XA_WIDTH + h * XA_HEAD_DIM:XA_WIDTH + (h + 1) * XA_HEAD_DIM]))
    o = jnp.concatenate(heads, axis=-1).astype(BF16)
    o_ref[...] = _ln_rows(ALPHA * x + _dot(o, wo_ref[...]), g_ref[...], b_ref[...])


def _xa(x, kv, wq, wo, g, b, seq, tm=512):
    t, d = x.shape
    nseq = seq // tm
    full = lambda a: pl.BlockSpec(a.shape, lambda i: (0,) * a.ndim)
    return pl.pallas_call(
        _xa_kernel,
        grid=(t // tm,),
        in_specs=[pl.BlockSpec((tm, d), lambda i: (i, 0)),
                  pl.BlockSpec((1,) + kv.shape[1:], lambda i: (i // nseq, 0, 0)),
                  full(wq), full(wo), full(g), full(b)],
        out_specs=pl.BlockSpec((tm, d), lambda i: (i, 0)),
        out_shape=jax.ShapeDtypeStruct((t, d), F32),
        compiler_params=_params("parallel"),
        name="xattn_ln",
    )(x, kv, wq, wo, g, b)


def _rope_tables128(seq):
    inv = ROPE_THETA ** (-jnp.arange(0, ROPE_DIM, 2, dtype=F32) / ROPE_DIM)
    ang = jnp.arange(seq, dtype=F32)[:, None] * inv[None, :]
    z = jnp.zeros((seq, 128 - ROPE_DIM), F32)
    cos, sin = jnp.cos(ang), jnp.sin(ang)
    return jnp.concatenate([cos, cos, z], axis=1), jnp.concatenate([sin, sin, z], axis=1)


def _rot_cols(w):
    half = w.shape[-1] // 2
    return jnp.concatenate([-w[..., half:], w[..., :half]], axis=-1)


def _pad_last(w, width):
    return jnp.pad(w, [(0, 0)] * (w.ndim - 1) + [(0, width - w.shape[-1])])


def _t5_bucket(dist):
    exact = REL_BUCKETS // 2
    df = jnp.maximum(dist, 1).astype(F32)
    large = exact + (jnp.log(df / exact) / math.log(REL_MAX_DIST / exact) * (REL_BUCKETS - exact)).astype(jnp.int32)
    large = jnp.minimum(large, REL_BUCKETS - 1)
    return jnp.where(dist < exact, dist, large)


def _dil_buckets(dil):
    qi = jnp.arange(DIL_BLK)[:, None]
    ki = jnp.arange(2 * DIL_BLK)[None, :]
    return _t5_bucket(jnp.maximum(qi + DIL_BLK - ki, 0) * dil).astype(jnp.int32)


def kernel(x, mem, ln_g, ln_b, ffn_wg, ffn_wu, ffn_wd, w_in, mla_q_norm, mla_kv_norm, mla_w_uq, mla_w_ukv,
           sgu_ln_g, sgu_ln_b, sgu_ws, sgu_bs, rel_bias, w_branch, w_out, xa_wq, xa_wkv, xa_wo):
    batch, seq, d = x.shape
    t = batch * seq
    mem_len = mem.shape[1]
    cos128, sin128 = _rope_tables128(seq)
    h = x.reshape(t, d)
    mem2 = mem.reshape(batch * mem_len, d)
    bf = lambda a: a.astype(BF16)
    row = lambda a: a.reshape(1, -1)

    o_cq = 0
    o_ckv = o_cq + Q_LORA
    o_kpe = o_ckv + KV_LORA
    o_z = o_kpe + ROPE_DIM
    o_qkv = o_z + 2 * SGU_WIDTH
    o_gate = o_qkv + 3 * DIL_HEADS * DIL_HEAD_DIM

    def ffn(h, l, i, k):
        return _ffn(h, bf(ffn_wg[l, i]), bf(ffn_wu[l, i]), bf(ffn_wd[l, i]), jnp.stack([ln_g[l, k], ln_b[l, k]]))

    buckets = [_dil_buckets(dil) for _, dil in DIL_PATTERNS]

    for l in range(DEPTH):
        h = ffn(h, l, 0, 0)

        wl = w_in[l]
        w_kpe = wl[:, o_kpe:o_z]
        w1 = bf(jnp.concatenate([wl[:, o_cq:o_kpe], _pad_last(w_kpe, 128), _pad_last(_rot_cols(w_kpe), 128)], axis=1))
        wuq = mla_w_uq[l].reshape(Q_LORA, MLA_HEADS, NOPE_DIM + ROPE_DIM)
        wqn = bf(wuq[:, :, :NOPE_DIM].reshape(Q_LORA, -1))
        wqr = bf(_pad_last(wuq[:, :, NOPE_DIM:], 128).reshape(Q_LORA, -1))
        wqrr = bf(_pad_last(_rot_cols(wuq[:, :, NOPE_DIM:]), 128).reshape(Q_LORA, -1))
        wukv = mla_w_ukv[l].reshape(KV_LORA, MLA_HEADS, NOPE_DIM + MLA_V_DIM)
        wk = bf(wukv[:, :, :NOPE_DIM].reshape(KV_LORA, -1))
        wv = bf(wukv[:, :, NOPE_DIM:].reshape(KV_LORA, -1))
        q, k, v = _mla_proj(h, w1, row(mla_q_norm[l]), row(mla_kv_norm[l]), wqn, wqr, wqrr, wk, wv, cos128, sin128, seq)
        hq = MLA_HEADS * MLA_QK_PAD
        o_a = _mla_attn(q.reshape(batch, seq, hq), k.reshape(batch, seq, hq), v.reshape(batch, seq, MLA_OUT))
        o_a = o_a.reshape(t, MLA_OUT)

        bs_b = jnp.broadcast_to(sgu_bs[l][:, :, None], (SGU_GROUPS, SGU_CHUNK, SGU_GROUP_DIM))
        o_b = _sgu(h, bf(wl[:, o_z:o_qkv]), row(sgu_ln_g[l]), row(sgu_ln_b[l]), sgu_ws[l], bs_b)

        qkv = _proj(h, bf(wl[:, o_qkv:o_gate]), BF16, tm=1024, tn=DIL_HEADS * DIL_HEAD_DIM)
        qkv = qkv.reshape(batch, seq, qkv.shape[1])
        outs, lses = [], []
        for gi, (window, dil) in enumerate(DIL_PATTERNS):
            o_g, l_g = _dil_attn(qkv, rel_bias, buckets[gi], gi, dil, window // dil)
            outs.append(o_g)
            lses.append(l_g)
        o_c = _dil_combine(outs, lses)

        h = _merge(h, o_a, o_b, o_c, bf(wl[:, o_gate:]), bf(w_branch[l]), bf(w_out[l]),
                   jnp.stack([ln_g[l, 1], ln_b[l, 1]]))

        kv = _proj(mem2, bf(xa_wkv[l]), BF16, tm=1024, tn=2 * XA_WIDTH).reshape(batch, mem_len, 2 * XA_WIDTH)
        h = _xa(h, kv, bf(xa_wq[l]), bf(xa_wo[l]), row(ln_g[l, 2]), row(ln_b[l, 2]), seq)

        h = ffn(h, l, 1, 3)
    return h.reshape(batch, seq, d)
```

```python
import functools
import math

import jax
import jax.numpy as jnp
import numpy as np
from jax import lax
from jax.experimental import pallas as pl
from jax.experimental.pallas import tpu as pltpu

F32 = jnp.float32
BF16 = jnp.bfloat16

D_MODEL = 2048
DEPTH = 2
D_FF = 5632
MLA_HEADS = 8
Q_LORA = 512
KV_LORA = 512
NOPE_DIM = 128
ROPE_DIM = 64
MLA_V_DIM = 128
MLA_QK_PAD = 256
ROPE_THETA = 10000.0
SGU_CHUNK = 128
SGU_GROUPS = 4
SGU_GROUP_DIM = 128
SGU_WIDTH = SGU_GROUPS * SGU_GROUP_DIM
DIL_PATTERNS = ((128, 1), (512, 4), (2048, 16))
DIL_HEADS_PER_GROUP = 4
DIL_HEAD_DIM = 128
DIL_HEADS = DIL_HEADS_PER_GROUP * len(DIL_PATTERNS)
DIL_OUT = DIL_HEADS_PER_GROUP * DIL_HEAD_DIM
DIL_BLK = 128
DIL_RES = 16
DIL_CHUNK = DIL_RES * DIL_RES
REL_BUCKETS = 32
REL_MAX_DIST = 2048
XA_HEADS = 4
XA_HEAD_DIM = 128
XA_WIDTH = XA_HEADS * XA_HEAD_DIM
MLA_OUT = MLA_HEADS * MLA_V_DIM
ALPHA = (2 * DEPTH) ** 0.25
LN_EPS = 1e-5
RMS_EPS = 1e-6
NEG_INF = -1e30

VMEM_LIMIT_BYTES = 56 * 1024 * 1024
FFN_TF = 512
MERGE_TN = 512
def _params(*semantics):
    return pltpu.CompilerParams(dimension_semantics=semantics, vmem_limit_bytes=VMEM_LIMIT_BYTES)


def _ln_rows(y, g, b):
    mu = jnp.mean(y, axis=-1, keepdims=True)
    yc = y - mu
    var = jnp.mean(yc * yc, axis=-1, keepdims=True)
    return yc * lax.rsqrt(var + LN_EPS) * g + b


def _rms_rows(y, g):
    return y * lax.rsqrt(jnp.mean(y * y, axis=-1, keepdims=True) + RMS_EPS) * g


def _dot(a, b):
    return jnp.dot(a, b, preferred_element_type=F32)


def _dot_nt(a, b):
    return lax.dot_general(a, b, (((1,), (1,)), ((), ())), preferred_element_type=F32)


def _ffn_kernel(x_hbm, wg_ref, wu_ref, wd_ref, gb_ref, o_ref, xbuf, xb_ref, sem, *, tm):
    i = pl.program_id(0)
    f = pl.program_id(1)

    def x_copy(tile):
        return pltpu.make_async_copy(x_hbm.at[pl.ds(tile * tm, tm), :], xbuf, sem)

    @pl.when(f == 0)
    def _():
        @pl.when(i == 0)
        def _():
            x_copy(0).start()

        x_copy(i).wait()
        x = xbuf[...]
        xb_ref[...] = x.astype(BF16)
        o_ref[...] = ALPHA * x

        @pl.when(i + 1 < pl.num_programs(0))
        def _():
            x_copy(i + 1).start()

    xb = xb_ref[...]
    gate = _dot(xb, wg_ref[...])
    up = _dot(xb, wu_ref[...])
    h = (0.5 * (gate * jax.nn.sigmoid(gate)) * up).astype(BF16)
    o_ref[...] += _dot(h, wd_ref[...])

    @pl.when(f == pl.num_programs(1) - 1)
    def _():
        o_ref[...] = _ln_rows(o_ref[...], gb_ref[0:1, :], gb_ref[1:2, :])


def _ffn(x, wg, wu, wd, gb, layer, half, tm=1024, tf=FFN_TF):
    t, d = x.shape
    return pl.pallas_call(
        functools.partial(_ffn_kernel, tm=tm),
        grid=(t // tm, wg.shape[3] // tf),
        in_specs=[
            pl.BlockSpec(memory_space=pl.ANY),
            pl.BlockSpec((None, None, d, tf), lambda i, f: (layer, half, 0, f)),
            pl.BlockSpec((None, None, d, tf), lambda i, f: (layer, half, 0, f)),
            pl.BlockSpec((None, None, tf, d), lambda i, f: (layer, half, f, 0)),
            pl.BlockSpec((2, d), lambda i, f: (0, 0)),
        ],
        out_specs=pl.BlockSpec((tm, d), lambda i, f: (i, 0)),
        out_shape=jax.ShapeDtypeStruct((t, d), F32),
        scratch_shapes=[pltpu.VMEM((tm, d), F32), pltpu.VMEM((tm, d), BF16), pltpu.SemaphoreType.DMA(())],
        compiler_params=_params("arbitrary", "arbitrary"),
        name="ffn_swiglu_ln",
    )(x, wg, wu, wd, gb)


def _proj_kernel(x_ref, w_ref, o_ref, *, w_transposed):
    dot = _dot_nt if w_transposed else _dot
    o_ref[...] = dot(x_ref[...].astype(BF16), w_ref[...]).astype(o_ref.dtype)


def _proj(x, w, layer, col0, n, out_dtype, tm, tn, w_transposed=False):
    t, k = x.shape
    tm = min(tm, t)
    assert t % tm == 0 and n % tn == 0 and col0 % tn == 0
    if w_transposed:
        w_spec = pl.BlockSpec((None, tn, k), lambda i, j: (layer, col0 // tn + j, 0))
    else:
        w_spec = pl.BlockSpec((None, k, tn), lambda i, j: (layer, 0, col0 // tn + j))
    return pl.pallas_call(
        functools.partial(_proj_kernel, w_transposed=w_transposed),
        grid=(t // tm, n // tn),
        in_specs=[pl.BlockSpec((tm, k), lambda i, j: (i, 0)), w_spec],
        out_specs=pl.BlockSpec((tm, tn), lambda i, j: (i, j)),
        out_shape=jax.ShapeDtypeStruct((t, n), out_dtype),
        compiler_params=_params("parallel", "arbitrary"),
        name="proj",
    )(x, w)


def _mla_proj_kernel(x_ref, w1_ref, qn_ref, kvn_ref, wqn_ref, wqr_ref, wqrr_ref, wk_ref, wv_ref,
                     cos_ref, sin_ref, q_ref, k_ref, v_ref):
    xb = x_ref[...].astype(BF16)
    c = _dot(xb, w1_ref[...])
    cos = cos_ref[...]
    sin = sin_ref[...]
    cq = _rms_rows(c[:, :Q_LORA], qn_ref[...]).astype(BF16)
    ckv = _rms_rows(c[:, Q_LORA:Q_LORA + KV_LORA], kvn_ref[...]).astype(BF16)
    off = Q_LORA + KV_LORA
    k_rope = (c[:, off:off + 128] * cos + c[:, off + 128:off + 256] * sin).astype(BF16)
    q_nope = _dot(cq, wqn_ref[...])
    q_pe = _dot(cq, wqr_ref[...])
    q_pe_rot = _dot(cq, wqrr_ref[...])
    k_nope = _dot(ckv, wk_ref[...])
    v_ref[...] = _dot(ckv, wv_ref[...]).astype(BF16)
    for h in range(MLA_HEADS):
        lo = h * MLA_QK_PAD
        hs = slice(h * 128, (h + 1) * 128)
        q_ref[:, lo:lo + 128] = q_nope[:, hs].astype(BF16)
        q_ref[:, lo + 128:lo + 256] = (q_pe[:, hs] * cos + q_pe_rot[:, hs] * sin).astype(BF16)
        k_ref[:, lo:lo + 128] = k_nope[:, hs].astype(BF16)
        k_ref[:, lo + 128:lo + 256] = k_rope


def _mla_proj(x, w1, qn, kvn, wqn, wqr, wqrr, wk, wv, cos128, sin128, seq, tm=512):
    t, d = x.shape
    nseq = seq // tm
    full = lambda a: pl.BlockSpec(a.shape, lambda i: (0,) * a.ndim)
    hq = MLA_HEADS * MLA_QK_PAD
    return pl.pallas_call(
        _mla_proj_kernel,
        grid=(t // tm,),
        in_specs=[
            pl.BlockSpec((tm, d), lambda i: (i, 0)),
            full(w1), full(qn), full(kvn), full(wqn), full(wqr), full(wqrr), full(wk), full(wv),
            pl.BlockSpec((tm, 128), lambda i: (i % nseq, 0)),
            pl.BlockSpec((tm, 128), lambda i: (i % nseq, 0)),
        ],
        out_specs=[
            pl.BlockSpec((tm, hq), lambda i: (i, 0)),
            pl.BlockSpec((tm, hq), lambda i: (i, 0)),
            pl.BlockSpec((tm, MLA_OUT), lambda i: (i, 0)),
        ],
        out_shape=[
            jax.ShapeDtypeStruct((t, hq), BF16),
            jax.ShapeDtypeStruct((t, hq), BF16),
            jax.ShapeDtypeStruct((t, MLA_OUT), BF16),
        ],
        compiler_params=_params("parallel"),
        name="mla_proj",
    )(x, w1, qn, kvn, wqn, wqr, wqrr, wk, wv, cos128, sin128)


def _mla_attn_kernel(q_ref, k_ref, v_ref, o_ref, *, tq, scale):
    seq = q_ref.shape[1]
    c = scale * math.log2(math.e)
    causal = (lax.broadcasted_iota(jnp.int32, (tq, tq), 1) <= lax.broadcasted_iota(jnp.int32, (tq, tq), 0))
    for qi in range(seq // tq):
        lo = qi * tq
        rows = slice(lo, lo + tq)
        q = q_ref[0, rows, :]
        s_d = jnp.where(causal, _dot_nt(q, k_ref[0, rows, :]), NEG_INF)
        m = jnp.max(s_d, axis=-1, keepdims=True)
        if qi > 0:
            s_p = _dot_nt(q, k_ref[0, :lo, :])
            m = jnp.maximum(m, jnp.max(s_p, axis=-1, keepdims=True))
            e_p = jnp.exp2((s_p - m) * c)
        e_d = jnp.exp2((s_d - m) * c)
        den = jnp.sum(e_d, axis=-1, keepdims=True)
        if qi > 0:
            den = den + jnp.sum(e_p, axis=-1, keepdims=True)
        inv = 1.0 / den
        o = _dot((e_d * inv).astype(BF16), v_ref[0, rows, :])
        if qi > 0:
            o = o + _dot((e_p * inv).astype(BF16), v_ref[0, :lo, :])
        o_ref[0, rows, :] = o.astype(o_ref.dtype)


def _mla_attn(q, k, v, tq=512):
    b, s, _ = q.shape
    kern = functools.partial(_mla_attn_kernel, tq=tq, scale=(NOPE_DIM + ROPE_DIM) ** -0.5)
    return pl.pallas_call(
        kern,
        grid=(b, MLA_HEADS),
        in_specs=[
            pl.BlockSpec((1, s, MLA_QK_PAD), lambda i, h: (i, 0, h)),
            pl.BlockSpec((1, s, MLA_QK_PAD), lambda i, h: (i, 0, h)),
            pl.BlockSpec((1, s, MLA_V_DIM), lambda i, h: (i, 0, h)),
        ],
        out_specs=pl.BlockSpec((1, s, MLA_V_DIM), lambda i, h: (i, 0, h)),
        out_shape=jax.ShapeDtypeStruct((b, s, MLA_OUT), BF16),
        compiler_params=_params("parallel", "parallel"),
        name="mla_attn",
    )(q, k, v)


def _sgu_kernel(x_ref, wu_ref, wv_ref, g_ref, b_ref, ws_ref, bs_ref, o_ref):
    tm = x_ref.shape[0]
    xb = x_ref[...].astype(BF16)
    gelu = lambda z: 0.5 * z * (1.0 + lax.erf(z * (2.0 ** -0.5)))
    u = gelu(_dot_nt(xb, wu_ref[...]))
    vn = _ln_rows(gelu(_dot_nt(xb, wv_ref[...])), g_ref[...], b_ref[...]).astype(BF16)
    row = lax.broadcasted_iota(jnp.int32, (SGU_CHUNK, SGU_CHUNK), 0)
    col = lax.broadcasted_iota(jnp.int32, (SGU_CHUNK, SGU_CHUNK), 1)
    for gi in range(SGU_GROUPS):
        w = jnp.where(col <= row, ws_ref[gi], 0.0).astype(BF16)
        bias = bs_ref[gi]
        cs = slice(gi * SGU_GROUP_DIM, (gi + 1) * SGU_GROUP_DIM)
        for ci in range(tm // SGU_CHUNK):
            rs = slice(ci * SGU_CHUNK, (ci + 1) * SGU_CHUNK)
            mixed = _dot(w, vn[rs, cs]) + bias
            o_ref[rs, cs] = (u[rs, cs] * mixed).astype(o_ref.dtype)


def _sgu(x, w, layer, col0, g, b, ws, bs_b, tm=512):
    t, d = x.shape
    assert col0 % SGU_WIDTH == 0
    full = lambda a: pl.BlockSpec(a.shape, lambda i: (0,) * a.ndim)
    half = lambda c: pl.BlockSpec((None, SGU_WIDTH, d), lambda i: (layer, col0 // SGU_WIDTH + c, 0))
    return pl.pallas_call(
        _sgu_kernel,
        grid=(t // tm,),
        in_specs=[pl.BlockSpec((tm, d), lambda i: (i, 0)), half(0), half(1), full(g), full(b), full(ws), full(bs_b)],
        out_specs=pl.BlockSpec((tm, SGU_WIDTH), lambda i: (i, 0)),
        out_shape=jax.ShapeDtypeStruct((t, SGU_WIDTH), BF16),
        compiler_params=_params("parallel"),
        name="sgu",
    )(x, w, w, g, b, ws, bs_b)


def _dil_pieces(dil, r, n):
    if dil == 1:
        return [(n * DIL_BLK, n * DIL_BLK, 1, DIL_BLK)]
    span = DIL_BLK * dil // DIL_CHUNK
    pieces = []
    for c in range(n * span, (n + 1) * span):
        for u in range(DIL_RES // dil):
            res = u * dil + r
            pieces.append((c * DIL_CHUNK + res * DIL_RES, c * DIL_CHUNK + res, DIL_RES, DIL_RES))
    return pieces


def _dil_attn_kernel(rel_ref, bucket_ref, dist_ref, perm_ref, q_ref, k_ref, v_ref, o_ref, l_ref, bias_ref, *stage,
                     dil, band, scale, group):
    seq = q_ref.shape[1]
    sub = seq // dil

    @pl.when(pl.program_id(0) == 0)
    def _():
        bucket = bucket_ref[...]
        for h in range(DIL_HEADS_PER_GROUP):
            acc = jnp.zeros(bucket.shape, F32)
            for bk in range(REL_BUCKETS):
                acc = jnp.where(bucket == bk, rel_ref[bk, group * DIL_HEADS_PER_GROUP + h], acc)
            bias_ref[h] = acc

    if dil > 1:
        perm = perm_ref[...]
        for src, dst in zip((q_ref, k_ref, v_ref), stage):
            for c in range(seq // DIL_CHUNK):
                cs = slice(c * DIL_CHUNK, (c + 1) * DIL_CHUNK)
                dst[cs, :] = _dot(perm, src[0, cs, :]).astype(BF16)

    def load(idx, h, r, blocks):
        cols = slice(h * DIL_HEAD_DIM, (h + 1) * DIL_HEAD_DIM)
        if dil == 1:
            return (q_ref, k_ref, v_ref)[idx][0, blocks[0] * DIL_BLK:(blocks[-1] + 1) * DIL_BLK, cols]
        parts = [stage[idx][row:row + size, cols] for n in blocks for row, _, _, size in _dil_pieces(dil, r, n)]
        return jnp.concatenate(parts, axis=0)

    dist = dist_ref[...]
    in_band = (dist >= 0) & (dist <= band)

    def attend(h, units, key_blocks, bias, mask):
        q = jnp.stack([load(0, h, r, [n]) for r, n in units])
        k = jnp.stack([load(1, h, r, key_blocks(n)) for r, n in units])
        v = jnp.stack([load(2, h, r, key_blocks(n)) for r, n in units])
        s = lax.dot_general(q, k, (((2,), (2,)), ((0,), (0,))), preferred_element_type=F32) * scale + bias
        s = jnp.where(mask, s, NEG_INF)
        m = jnp.max(s, axis=-1, keepdims=True)
        e = jnp.exp(s - m)
        den = jnp.sum(e, axis=-1, keepdims=True)
        p = (e * (1.0 / den)).astype(BF16)
        o = lax.dot_general(p, v, (((2,), (1,)), ((0,), (0,))), preferred_element_type=F32)
        lse = jnp.broadcast_to(m + jnp.log(den), o.shape)
        for u, (r, n) in enumerate(units):
            at = 0
            for _, tok, stride, size in _dil_pieces(dil, r, n):
                rows = pl.ds(tok, size) if stride == 1 else pl.ds(tok, size, stride=stride)
                o_ref[0, h, rows, :] = o[u, at:at + size]
                l_ref[0, h, rows, :] = lse[u, at:at + size]
                at += size

    nblk = sub // DIL_BLK
    for h in range(DIL_HEADS_PER_GROUP):
        bias = bias_ref[h]
        first = [(r, 0) for r in range(dil)]
        rest = [(r, n) for r in range(dil) for n in range(1, nblk)]
        if dil == 1:
            groups = [[u] for u in first + rest]
        else:
            groups = [first, rest] if rest else [first]
        for units in groups:
            if units[0][1] == 0:
                attend(h, units, lambda n: [n], bias[:, DIL_BLK:], in_band[:, DIL_BLK:])
            else:
                attend(h, units, lambda n: [n - 1, n], bias, in_band)


def _dil_attn(qkv, rel_bias, bucket, dist, perm, group, dil, band):
    b, seq, _ = qkv.shape
    ngroups = len(DIL_PATTERNS)
    kern = functools.partial(_dil_attn_kernel, dil=dil, band=band, scale=DIL_HEAD_DIM ** -0.5, group=group)
    full = lambda a: pl.BlockSpec(a.shape, lambda i: (0, 0))

    def in_blk(c):
        return pl.BlockSpec((1, seq, DIL_OUT), lambda i: (i, 0, c * ngroups + group))

    out_shape = (b, DIL_HEADS_PER_GROUP, seq, DIL_HEAD_DIM)
    out_blk = pl.BlockSpec((1,) + out_shape[1:], lambda i: (i, 0, 0, 0))
    stage = [pltpu.VMEM((seq, DIL_OUT), BF16)] * 3 if dil > 1 else []
    return pl.pallas_call(
        kern,
        grid=(b,),
        in_specs=[pl.BlockSpec(memory_space=pltpu.SMEM), full(bucket), full(dist), full(perm),
                  in_blk(0), in_blk(1), in_blk(2)],
        out_specs=[out_blk, out_blk],
        out_shape=[jax.ShapeDtypeStruct(out_shape, F32)] * 2,
        scratch_shapes=[pltpu.VMEM((DIL_HEADS_PER_GROUP, DIL_BLK, 2 * DIL_BLK), F32)] + stage,
        compiler_params=_params("arbitrary"),
        name="dil_attn",
    )(rel_bias, bucket, dist, perm, qkv, qkv, qkv)


def _dil_combine_kernel(o0, o1, o2, l0, l1, l2, out_ref):
    for h in range(DIL_HEADS_PER_GROUP):
        a0, a1, a2 = l0[0, h], l1[0, h], l2[0, h]
        m = jnp.maximum(jnp.maximum(a0, a1), a2)
        e0, e1, e2 = jnp.exp(a0 - m), jnp.exp(a1 - m), jnp.exp(a2 - m)
        inv = 1.0 / (e0 + e1 + e2)
        o = (e0 * inv) * o0[0, h] + (e1 * inv) * o1[0, h] + (e2 * inv) * o2[0, h]
        out_ref[:, h * DIL_HEAD_DIM:(h + 1) * DIL_HEAD_DIM] = o.astype(out_ref.dtype)


def _dil_combine(outs, lses, ts=1024):
    b, nh, seq, hd = outs[0].shape
    nseq = seq // ts
    blk = pl.BlockSpec((1, nh, ts, hd), lambda i, j: (i, 0, j, 0))
    return pl.pallas_call(
        _dil_combine_kernel,
        grid=(b, nseq),
        in_specs=[blk] * 6,
        out_specs=pl.BlockSpec((ts, nh * hd), lambda i, j: (i * nseq + j, 0)),
        out_shape=jax.ShapeDtypeStruct((b * seq, nh * hd), BF16),
        compiler_params=_params("parallel", "parallel"),
        name="dil_combine",
    )(*outs, *lses)


def _merge_kernel(x_ref, oa_ref, ob_ref, oc_ref, wga_ref, wgb_ref, wgc_ref, p_ref, wo_ref, gb_ref, o_ref, xb_ref):
    j = pl.program_id(1)

    @pl.when(j == 0)
    def _():
        x = x_ref[...]
        xb_ref[...] = x.astype(BF16)
        o_ref[...] = ALPHA * x

    xb = xb_ref[...]
    o_sgu = MLA_OUT + SGU_WIDTH
    merged = jax.nn.sigmoid(_dot_nt(xb, wga_ref[...])) * _dot(oa_ref[...], p_ref[:MLA_OUT, :])
    merged += jax.nn.sigmoid(_dot_nt(xb, wgb_ref[...])) * _dot(ob_ref[...], p_ref[MLA_OUT:o_sgu, :])
    merged += jax.nn.sigmoid(_dot_nt(xb, wgc_ref[...])) * _dot(oc_ref[...], p_ref[o_sgu:, :])
    o_ref[...] += _dot(merged.astype(BF16), wo_ref[...])

    @pl.when(j == pl.num_programs(1) - 1)
    def _():
        o_ref[...] = _ln_rows(o_ref[...], gb_ref[0:1, :], gb_ref[1:2, :])


def _merge(x, oa, ob, oc, wg, layer, col0, p, wo, gb, tm=512, tn=MERGE_TN):
    t, d = x.shape
    nj = d // tn
    assert col0 % tn == 0
    rows = lambda a: pl.BlockSpec((tm, a.shape[1]), lambda i, j: (i, 0))
    gate = lambda c: pl.BlockSpec((None, tn, d), lambda i, j: (layer, col0 // tn + c * nj + j, 0))
    return pl.pallas_call(
        _merge_kernel,
        grid=(t // tm, nj),
        in_specs=[rows(x), rows(oa), rows(ob), rows(oc), gate(0), gate(1), gate(2),
                  pl.BlockSpec((None, p.shape[1], tn), lambda i, j: (layer, 0, j)),
                  pl.BlockSpec((None, tn, d), lambda i, j: (layer, j, 0)),
                  pl.BlockSpec((2, d), lambda i, j: (0, 0))],
        out_specs=pl.BlockSpec((tm, d), lambda i, j: (i, 0)),
        out_shape=jax.ShapeDtypeStruct((t, d), F32),
        scratch_shapes=[pltpu.VMEM((tm, d), BF16)],
        compiler_params=_params("parallel", "arbitrary"),
        name="merge_out_ln",
    )(x, oa, ob, oc, wg, wg, wg, p, wo, gb)


def _xa_kernel(x_ref, kv_ref, wq_ref, wo_ref, g_ref, b_ref, o_ref):
    x = x_ref[...]
    q = _dot(x.astype(BF16), wq_ref[...]).astype(BF16)
    heads = []
    for h in range(XA_HEADS):
        cs = slice(h * XA_HEAD_DIM, (h + 1) * XA_HEAD_DIM)
        s = _dot_nt(q[:, cs], kv_ref[0, :, cs]) * (XA_HEAD_DIM ** -0.5)
        m = jnp.max(s, axis=-1, keepdims=True)
        e = jnp.exp(s - m)
        p = (e * (1.0 / jnp.sum(e, axis=-1, keepdims=True))).astype(BF16)
        heads.append(_dot(p, kv_ref[0, :, XA_WIDTH + h * XA_HEAD_DIM:XA_WIDTH + (h + 1) * XA_HEAD_DIM]))
    o = jnp.concatenate(heads, axis=-1).astype(BF16)
    o_ref[...] = _ln_rows(ALPHA * x + _dot(o, wo_ref[...]), g_ref[...], b_ref[...])


def _xa(x, kv, wq, wo, layer, g, b, seq, tm=512):
    t, d = x.shape
    nseq = seq // tm
    full = lambda a: pl.BlockSpec(a.shape, lambda i: (0,) * a.ndim)
    of_layer = lambda a: pl.BlockSpec((None,) + a.shape[1:], lambda i: (layer, 0, 0))
    return pl.pallas_call(
        _xa_kernel,
        grid=(t // tm,),
        in_specs=[pl.BlockSpec((tm, d), lambda i: (i, 0)),
                  pl.BlockSpec((1,) + kv.shape[1:], lambda i: (i // nseq, 0, 0)),
                  of_layer(wq), of_layer(wo), full(g), full(b)],
        out_specs=pl.BlockSpec((tm, d), lambda i: (i, 0)),
        out_shape=jax.ShapeDtypeStruct((t, d), F32),
        compiler_params=_params("parallel"),
        name="xattn_ln",
    )(x, kv, wq, wo, g, b)


def _rope_tables128(seq):
    inv = ROPE_THETA ** (-jnp.arange(0, ROPE_DIM, 2, dtype=F32) / ROPE_DIM)
    ang = jnp.arange(seq, dtype=F32)[:, None] * inv[None, :]
    z = jnp.zeros((seq, 128 - ROPE_DIM), F32)
    cos, sin = jnp.cos(ang), jnp.sin(ang)
    return jnp.concatenate([cos, cos, z], axis=1), jnp.concatenate([sin, sin, z], axis=1)


def _rot_cols(w):
    half = w.shape[-1] // 2
    return jnp.concatenate([-w[..., half:], w[..., :half]], axis=-1)


def _pad_last(w, width):
    return jnp.pad(w, [(0, 0)] * (w.ndim - 1) + [(0, width - w.shape[-1])])


def _t5_bucket(dist):
    exact = REL_BUCKETS // 2
    df = jnp.maximum(dist, 1).astype(F32)
    large = exact + (jnp.log(df / exact) / math.log(REL_MAX_DIST / exact) * (REL_BUCKETS - exact)).astype(jnp.int32)
    large = jnp.minimum(large, REL_BUCKETS - 1)
    return jnp.where(dist < exact, dist, large)


def _dil_tables(dil):
    pos = np.concatenate([(tok + stride * np.arange(size)) // dil for _, tok, stride, size in _dil_pieces(dil, 0, 0)])
    key_pos = np.concatenate([pos, pos + DIL_BLK])
    dist = jnp.asarray(pos[:, None] + DIL_BLK - key_pos[None, :], jnp.int32)
    return dist, _t5_bucket(jnp.maximum(dist, 0) * dil).astype(jnp.int32)


def _dil_perm():
    a = np.arange(DIL_CHUNK)
    perm = np.zeros((DIL_CHUNK, DIL_CHUNK), np.float32)
    perm[a, (a % DIL_RES) * DIL_RES + a // DIL_RES] = 1.0
    return jnp.asarray(perm, BF16)


def kernel(x, mem, ln_g, ln_b, ffn_wg, ffn_wu, ffn_wd, w_in, mla_q_norm, mla_kv_norm, mla_w_uq, mla_w_ukv,
           sgu_ln_g, sgu_ln_b, sgu_ws, sgu_bs, rel_bias, w_branch, w_out, xa_wq, xa_wkv, xa_wo):
    batch, seq, d = x.shape
    t = batch * seq
    mem_len = mem.shape[1]
    cos128, sin128 = _rope_tables128(seq)
    h = x.reshape(t, d)
    mem2 = mem.reshape(batch * mem_len, d)
    bf = lambda a: a.astype(BF16)
    row = lambda a: a.reshape(1, -1)

    o_cq = 0
    o_ckv = o_cq + Q_LORA
    o_kpe = o_ckv + KV_LORA
    o_z = o_kpe + ROPE_DIM
    o_qkv = o_z + 2 * SGU_WIDTH
    o_gate = o_qkv + 3 * DIL_HEADS * DIL_HEAD_DIM

    wg_b, wu_b, wd_b = bf(ffn_wg), bf(ffn_wu), bf(ffn_wd)
    w_in_t = jnp.swapaxes(w_in, 1, 2)
    w_wide = bf(jnp.concatenate([w_in_t[:, o_qkv:o_gate], w_in_t[:, o_gate:], w_in_t[:, o_z:o_qkv]], axis=1))
    w_head = jnp.swapaxes(w_in_t[:, :o_z], 1, 2)
    c_qkv, c_gate, c_sgu = 0, o_gate - o_qkv, w_in.shape[2] - o_qkv
    w_branch_b, w_out_b = bf(w_branch), bf(w_out)
    xa_wq_b, xa_wkv_b, xa_wo_b = bf(xa_wq), bf(xa_wkv), bf(xa_wo)

    def ffn(h, l, i, k):
        return _ffn(h, wg_b, wu_b, wd_b, jnp.stack([ln_g[l, k], ln_b[l, k]]), l, i)

    dil_tables = [_dil_tables(dil) for _, dil in DIL_PATTERNS]
    dil_perm = _dil_perm()

    for l in range(DEPTH):
        h = ffn(h, l, 0, 0)

        wl = w_head[l]
        w_kpe = wl[:, o_kpe:o_z]
        w1 = bf(jnp.concatenate([wl[:, o_cq:o_kpe], _pad_last(w_kpe, 128), _pad_last(_rot_cols(w_kpe), 128)], axis=1))
        wuq = mla_w_uq[l].reshape(Q_LORA, MLA_HEADS, NOPE_DIM + ROPE_DIM)
        wqn = bf(wuq[:, :, :NOPE_DIM].reshape(Q_LORA, -1))
        wqr = bf(_pad_last(wuq[:, :, NOPE_DIM:], 128).reshape(Q_LORA, -1))
        wqrr = bf(_pad_last(_rot_cols(wuq[:, :, NOPE_DIM:]), 128).reshape(Q_LORA, -1))
        wukv = mla_w_ukv[l].reshape(KV_LORA, MLA_HEADS, NOPE_DIM + MLA_V_DIM)
        wk = bf(wukv[:, :, :NOPE_DIM].reshape(KV_LORA, -1))
        wv = bf(wukv[:, :, NOPE_DIM:].reshape(KV_LORA, -1))
        q, k, v = _mla_proj(h, w1, row(mla_q_norm[l]), row(mla_kv_norm[l]), wqn, wqr, wqrr, wk, wv, cos128, sin128, seq)
        hq = MLA_HEADS * MLA_QK_PAD
        o_a = _mla_attn(q.reshape(batch, seq, hq), k.reshape(batch, seq, hq), v.reshape(batch, seq, MLA_OUT))
        o_a = o_a.reshape(t, MLA_OUT)

        bs_b = jnp.broadcast_to(sgu_bs[l][:, :, None], (SGU_GROUPS, SGU_CHUNK, SGU_GROUP_DIM))
        o_b = _sgu(h, w_wide, l, c_sgu, row(sgu_ln_g[l]), row(sgu_ln_b[l]), sgu_ws[l], bs_b)

        qkv = _proj(h, w_wide, l, c_qkv, o_gate - o_qkv, BF16, tm=1024, tn=DIL_HEADS * DIL_HEAD_DIM,
                    w_transposed=True)
        qkv = qkv.reshape(batch, seq, qkv.shape[1])
        outs, lses = [], []
        for gi, (window, dil) in enumerate(DIL_PATTERNS):
            dist, bucket = dil_tables[gi]
            o_g, l_g = _dil_attn(qkv, rel_bias, bucket, dist, dil_perm, gi, dil, window // dil)
            outs.append(o_g)
            lses.append(l_g)
        o_c = _dil_combine(outs, lses)

        h = _merge(h, o_a, o_b, o_c, w_wide, l, c_gate, w_branch_b, w_out_b, jnp.stack([ln_g[l, 1], ln_b[l, 1]]))

        kv = _proj(mem2, xa_wkv_b, l, 0, 2 * XA_WIDTH, BF16, tm=1024, tn=2 * XA_WIDTH)
        kv = kv.reshape(batch, mem_len, 2 * XA_WIDTH)
        h = _xa(h, kv, xa_wq_b, xa_wo_b, l, row(ln_g[l, 2]), row(ln_b[l, 2]), seq)

        h = ffn(h, l, 1, 3)
    return h.reshape(batch, seq, d)
```

```python
import functools
import math

import jax
import jax.numpy as jnp
import numpy as np
from jax import lax
from jax.experimental import pallas as pl
from jax.experimental.pallas import tpu as pltpu

F32 = jnp.float32
BF16 = jnp.bfloat16

D_MODEL = 2048
DEPTH = 2
D_FF = 5632
MLA_HEADS = 8
Q_LORA = 512
KV_LORA = 512
NOPE_DIM = 128
ROPE_DIM = 64
MLA_V_DIM = 128
MLA_QK_PAD = 256
ROPE_THETA = 10000.0
SGU_CHUNK = 128
SGU_GROUPS = 4
SGU_GROUP_DIM = 128
SGU_WIDTH = SGU_GROUPS * SGU_GROUP_DIM
DIL_PATTERNS = ((128, 1), (512, 4), (2048, 16))
DIL_HEADS_PER_GROUP = 4
DIL_HEAD_DIM = 128
DIL_HEADS = DIL_HEADS_PER_GROUP * len(DIL_PATTERNS)
DIL_OUT = DIL_HEADS_PER_GROUP * DIL_HEAD_DIM
DIL_BLK = 128
DIL_RES = 16
DIL_CHUNK = DIL_RES * DIL_RES
REL_BUCKETS = 32
REL_MAX_DIST = 2048
XA_HEADS = 4
XA_HEAD_DIM = 128
XA_WIDTH = XA_HEADS * XA_HEAD_DIM
MLA_OUT = MLA_HEADS * MLA_V_DIM
ALPHA = (2 * DEPTH) ** 0.25
LN_EPS = 1e-5
RMS_EPS = 1e-6
NEG_INF = -1e30

VMEM_LIMIT_BYTES = 56 * 1024 * 1024
FFN_TF = 512
MERGE_TN = 512
BF16_SUBLANE_TILE = 16
def _params(*semantics):
    return pltpu.CompilerParams(dimension_semantics=semantics, vmem_limit_bytes=VMEM_LIMIT_BYTES)


def _ln_rows(y, g, b):
    mu = jnp.mean(y, axis=-1, keepdims=True)
    yc = y - mu
    var = jnp.mean(yc * yc, axis=-1, keepdims=True)
    return yc * lax.rsqrt(var + LN_EPS) * g + b


def _rms_rows(y, g):
    return y * lax.rsqrt(jnp.mean(y * y, axis=-1, keepdims=True) + RMS_EPS) * g


def _row_block(layer, th, k, row_of):
    return pl.BlockSpec((pl.Squeezed(), pl.Element(th), pl.Element(k)),
                        lambda *ids: (layer, pl.multiple_of(row_of(*ids), BF16_SUBLANE_TILE), 0))


def _dot(a, b):
    return jnp.dot(a, b, preferred_element_type=F32)


def _dot_nt(a, b):
    return lax.dot_general(a, b, (((1,), (1,)), ((), ())), preferred_element_type=F32)


def _ffn_kernel(x_hbm, wg_ref, wu_ref, wd_ref, gb_ref, o_ref, xbuf, xb_ref, sem, *, tm):
    i = pl.program_id(0)
    f = pl.program_id(1)

    def x_copy(tile):
        return pltpu.make_async_copy(x_hbm.at[pl.ds(tile * tm, tm), :], xbuf, sem)

    @pl.when(f == 0)
    def _():
        @pl.when(i == 0)
        def _():
            x_copy(0).start()

        x_copy(i).wait()
        x = xbuf[...]
        xb_ref[...] = x.astype(BF16)
        o_ref[...] = ALPHA * x

        @pl.when(i + 1 < pl.num_programs(0))
        def _():
            x_copy(i + 1).start()

    xb = xb_ref[...]
    gate = _dot(xb, wg_ref[...])
    up = _dot(xb, wu_ref[...])
    h = (0.5 * (gate * jax.nn.sigmoid(gate)) * up).astype(BF16)
    o_ref[...] += _dot(h, wd_ref[...])

    @pl.when(f == pl.num_programs(1) - 1)
    def _():
        o_ref[...] = _ln_rows(o_ref[...], gb_ref[0:1, :], gb_ref[1:2, :])


def _ffn(x, wg, wu, wd, gb, layer, half, tm=1024, tf=FFN_TF):
    t, d = x.shape
    return pl.pallas_call(
        functools.partial(_ffn_kernel, tm=tm),
        grid=(t // tm, wg.shape[3] // tf),
        in_specs=[
            pl.BlockSpec(memory_space=pl.ANY),
            pl.BlockSpec((None, None, d, tf), lambda i, f: (layer, half, 0, f)),
            pl.BlockSpec((None, None, d, tf), lambda i, f: (layer, half, 0, f)),
            pl.BlockSpec((None, None, tf, d), lambda i, f: (layer, half, f, 0)),
            pl.BlockSpec((2, d), lambda i, f: (0, 0)),
        ],
        out_specs=pl.BlockSpec((tm, d), lambda i, f: (i, 0)),
        out_shape=jax.ShapeDtypeStruct((t, d), F32),
        scratch_shapes=[pltpu.VMEM((tm, d), F32), pltpu.VMEM((tm, d), BF16), pltpu.SemaphoreType.DMA(())],
        compiler_params=_params("arbitrary", "arbitrary"),
        name="ffn_swiglu_ln",
    )(x, wg, wu, wd, gb)


def _proj_kernel(x_ref, w_ref, o_ref, *, w_transposed):
    dot = _dot_nt if w_transposed else _dot
    o_ref[...] = dot(x_ref[...].astype(BF16), w_ref[...]).astype(o_ref.dtype)


def _proj(x, w, layer, col0, n, out_dtype, tm, tn, w_transposed=False):
    t, k = x.shape
    tm = min(tm, t)
    assert t % tm == 0 and n % tn == 0
    if w_transposed:
        w_spec = _row_block(layer, tn, k, lambda i, j: col0 + j * tn)
    else:
        assert col0 % tn == 0
        w_spec = pl.BlockSpec((None, k, tn), lambda i, j: (layer, 0, col0 // tn + j))
    return pl.pallas_call(
        functools.partial(_proj_kernel, w_transposed=w_transposed),
        grid=(t // tm, n // tn),
        in_specs=[pl.BlockSpec((tm, k), lambda i, j: (i, 0)), w_spec],
        out_specs=pl.BlockSpec((tm, tn), lambda i, j: (i, j)),
        out_shape=jax.ShapeDtypeStruct((t, n), out_dtype),
        compiler_params=_params("parallel", "arbitrary"),
        name="proj",
    )(x, w)


def _mla_proj_kernel(x_ref, w1_ref, qn_ref, kvn_ref, wqn_ref, wqr_ref, wqrr_ref, wk_ref, wv_ref,
                     cos_ref, sin_ref, q_ref, k_ref, v_ref):
    xb = x_ref[...].astype(BF16)
    c = _dot(xb, w1_ref[...])
    cos = cos_ref[...]
    sin = sin_ref[...]
    cq = _rms_rows(c[:, :Q_LORA], qn_ref[...]).astype(BF16)
    ckv = _rms_rows(c[:, Q_LORA:Q_LORA + KV_LORA], kvn_ref[...]).astype(BF16)
    off = Q_LORA + KV_LORA
    k_rope = (c[:, off:off + 128] * cos + c[:, off + 128:off + 256] * sin).astype(BF16)
    q_nope = _dot(cq, wqn_ref[...])
    q_pe = _dot(cq, wqr_ref[...])
    q_pe_rot = _dot(cq, wqrr_ref[...])
    k_nope = _dot(ckv, wk_ref[...])
    v_ref[...] = _dot(ckv, wv_ref[...]).astype(BF16)
    for h in range(MLA_HEADS):
        lo = h * MLA_QK_PAD
        hs = slice(h * 128, (h + 1) * 128)
        q_ref[:, lo:lo + 128] = q_nope[:, hs].astype(BF16)
        q_ref[:, lo + 128:lo + 256] = (q_pe[:, hs] * cos + q_pe_rot[:, hs] * sin).astype(BF16)
        k_ref[:, lo:lo + 128] = k_nope[:, hs].astype(BF16)
        k_ref[:, lo + 128:lo + 256] = k_rope


def _mla_proj(x, w1, qn, kvn, wqn, wqr, wqrr, wk, wv, cos128, sin128, seq, tm=512):
    t, d = x.shape
    nseq = seq // tm
    full = lambda a: pl.BlockSpec(a.shape, lambda i: (0,) * a.ndim)
    hq = MLA_HEADS * MLA_QK_PAD
    return pl.pallas_call(
        _mla_proj_kernel,
        grid=(t // tm,),
        in_specs=[
            pl.BlockSpec((tm, d), lambda i: (i, 0)),
            full(w1), full(qn), full(kvn), full(wqn), full(wqr), full(wqrr), full(wk), full(wv),
            pl.BlockSpec((tm, 128), lambda i: (i % nseq, 0)),
            pl.BlockSpec((tm, 128), lambda i: (i % nseq, 0)),
        ],
        out_specs=[
            pl.BlockSpec((tm, hq), lambda i: (i, 0)),
            pl.BlockSpec((tm, hq), lambda i: (i, 0)),
            pl.BlockSpec((tm, MLA_OUT), lambda i: (i, 0)),
        ],
        out_shape=[
            jax.ShapeDtypeStruct((t, hq), BF16),
            jax.ShapeDtypeStruct((t, hq), BF16),
            jax.ShapeDtypeStruct((t, MLA_OUT), BF16),
        ],
        compiler_params=_params("parallel"),
        name="mla_proj",
    )(x, w1, qn, kvn, wqn, wqr, wqrr, wk, wv, cos128, sin128)


def _mla_attn_kernel(q_ref, k_ref, v_ref, o_ref, *, tq, scale):
    seq = q_ref.shape[1]
    c = scale * math.log2(math.e)
    causal = (lax.broadcasted_iota(jnp.int32, (tq, tq), 1) <= lax.broadcasted_iota(jnp.int32, (tq, tq), 0))
    for qi in range(seq // tq):
        lo = qi * tq
        rows = slice(lo, lo + tq)
        q = q_ref[0, rows, :]
        s_d = jnp.where(causal, _dot_nt(q, k_ref[0, rows, :]), NEG_INF)
        m = jnp.max(s_d, axis=-1, keepdims=True)
        if qi > 0:
            s_p = _dot_nt(q, k_ref[0, :lo, :])
            m = jnp.maximum(m, jnp.max(s_p, axis=-1, keepdims=True))
            e_p = jnp.exp2((s_p - m) * c)
        e_d = jnp.exp2((s_d - m) * c)
        den = jnp.sum(e_d, axis=-1, keepdims=True)
        if qi > 0:
            den = den + jnp.sum(e_p, axis=-1, keepdims=True)
        inv = 1.0 / den
        o = _dot((e_d * inv).astype(BF16), v_ref[0, rows, :])
        if qi > 0:
            o = o + _dot((e_p * inv).astype(BF16), v_ref[0, :lo, :])
        o_ref[0, rows, :] = o.astype(o_ref.dtype)


def _mla_attn(q, k, v, tq=512):
    b, s, _ = q.shape
    kern = functools.partial(_mla_attn_kernel, tq=tq, scale=(NOPE_DIM + ROPE_DIM) ** -0.5)
    return pl.pallas_call(
        kern,
        grid=(b, MLA_HEADS),
        in_specs=[
            pl.BlockSpec((1, s, MLA_QK_PAD), lambda i, h: (i, 0, h)),
            pl.BlockSpec((1, s, MLA_QK_PAD), lambda i, h: (i, 0, h)),
            pl.BlockSpec((1, s, MLA_V_DIM), lambda i, h: (i, 0, h)),
        ],
        out_specs=pl.BlockSpec((1, s, MLA_V_DIM), lambda i, h: (i, 0, h)),
        out_shape=jax.ShapeDtypeStruct((b, s, MLA_OUT), BF16),
        compiler_params=_params("parallel", "parallel"),
        name="mla_attn",
    )(q, k, v)


def _sgu_kernel(x_ref, wu_ref, wv_ref, g_ref, b_ref, ws_ref, bs_ref, o_ref):
    tm = x_ref.shape[0]
    xb = x_ref[...].astype(BF16)
    gelu = lambda z: 0.5 * z * (1.0 + lax.erf(z * (2.0 ** -0.5)))
    u = gelu(_dot_nt(xb, wu_ref[...]))
    vn = _ln_rows(gelu(_dot_nt(xb, wv_ref[...])), g_ref[...], b_ref[...]).astype(BF16)
    row = lax.broadcasted_iota(jnp.int32, (SGU_CHUNK, SGU_CHUNK), 0)
    col = lax.broadcasted_iota(jnp.int32, (SGU_CHUNK, SGU_CHUNK), 1)
    for gi in range(SGU_GROUPS):
        w = jnp.where(col <= row, ws_ref[gi], 0.0).astype(BF16)
        bias = bs_ref[gi]
        cs = slice(gi * SGU_GROUP_DIM, (gi + 1) * SGU_GROUP_DIM)
        for ci in range(tm // SGU_CHUNK):
            rs = slice(ci * SGU_CHUNK, (ci + 1) * SGU_CHUNK)
            mixed = _dot(w, vn[rs, cs]) + bias
            o_ref[rs, cs] = (u[rs, cs] * mixed).astype(o_ref.dtype)


def _sgu(x, w, layer, col0, g, b, ws, bs_b, tm=512):
    t, d = x.shape
    full = lambda a: pl.BlockSpec(a.shape, lambda i: (0,) * a.ndim)
    half = lambda c: _row_block(layer, SGU_WIDTH, d, lambda i: col0 + c * SGU_WIDTH)
    return pl.pallas_call(
        _sgu_kernel,
        grid=(t // tm,),
        in_specs=[pl.BlockSpec((tm, d), lambda i: (i, 0)), half(0), half(1), full(g), full(b), full(ws), full(bs_b)],
        out_specs=pl.BlockSpec((tm, SGU_WIDTH), lambda i: (i, 0)),
        out_shape=jax.ShapeDtypeStruct((t, SGU_WIDTH), BF16),
        compiler_params=_params("parallel"),
        name="sgu",
    )(x, w, w, g, b, ws, bs_b)


def _dil_pieces(dil, r, n):
    if dil == 1:
        return [(n * DIL_BLK, n * DIL_BLK, 1, DIL_BLK)]
    span = DIL_BLK * dil // DIL_CHUNK
    pieces = []
    for c in range(n * span, (n + 1) * span):
        for u in range(DIL_RES // dil):
            res = u * dil + r
            pieces.append((c * DIL_CHUNK + res * DIL_RES, c * DIL_CHUNK + res, DIL_RES, DIL_RES))
    return pieces


def _dil_attn_kernel(rel_ref, bucket_ref, dist_ref, perm_ref, q_ref, k_ref, v_ref, o_ref, l_ref, bias_ref, *stage,
                     dil, band, scale, group):
    seq = q_ref.shape[1]
    sub = seq // dil

    @pl.when(pl.program_id(0) == 0)
    def _():
        bucket = bucket_ref[...]
        for h in range(DIL_HEADS_PER_GROUP):
            acc = jnp.zeros(bucket.shape, F32)
            for bk in range(REL_BUCKETS):
                acc = jnp.where(bucket == bk, rel_ref[bk, group * DIL_HEADS_PER_GROUP + h], acc)
            bias_ref[h] = acc

    if dil > 1:
        perm = perm_ref[...]
        for src, dst in zip((q_ref, k_ref, v_ref), stage):
            for c in range(seq // DIL_CHUNK):
                cs = slice(c * DIL_CHUNK, (c + 1) * DIL_CHUNK)
                dst[cs, :] = _dot(perm, src[0, cs, :]).astype(BF16)

    def load(idx, h, r, blocks):
        cols = slice(h * DIL_HEAD_DIM, (h + 1) * DIL_HEAD_DIM)
        if dil == 1:
            return (q_ref, k_ref, v_ref)[idx][0, blocks[0] * DIL_BLK:(blocks[-1] + 1) * DIL_BLK, cols]
        parts = [stage[idx][row:row + size, cols] for n in blocks for row, _, _, size in _dil_pieces(dil, r, n)]
        return jnp.concatenate(parts, axis=0)

    dist = dist_ref[...]
    in_band = (dist >= 0) & (dist <= band)

    def attend(h, units, key_blocks, bias, mask):
        q = jnp.stack([load(0, h, r, [n]) for r, n in units])
        k = jnp.stack([load(1, h, r, key_blocks(n)) for r, n in units])
        v = jnp.stack([load(2, h, r, key_blocks(n)) for r, n in units])
        s = lax.dot_general(q, k, (((2,), (2,)), ((0,), (0,))), preferred_element_type=F32) * scale + bias
        s = jnp.where(mask, s, NEG_INF)
        m = jnp.max(s, axis=-1, keepdims=True)
        e = jnp.exp(s - m)
        den = jnp.sum(e, axis=-1, keepdims=True)
        p = (e * (1.0 / den)).astype(BF16)
        o = lax.dot_general(p, v, (((2,), (1,)), ((0,), (0,))), preferred_element_type=F32)
        lse = jnp.broadcast_to(m + jnp.log(den), o.shape)
        for u, (r, n) in enumerate(units):
            at = 0
            for _, tok, stride, size in _dil_pieces(dil, r, n):
                rows = pl.ds(tok, size) if stride == 1 else pl.ds(tok, size, stride=stride)
                o_ref[0, h, rows, :] = o[u, at:at + size]
                l_ref[0, h, rows, :] = lse[u, at:at + size]
                at += size

    nblk = sub // DIL_BLK
    for h in range(DIL_HEADS_PER_GROUP):
        bias = bias_ref[h]
        first = [(r, 0) for r in range(dil)]
        rest = [(r, n) for r in range(dil) for n in range(1, nblk)]
        if dil == 1:
            groups = [[u] for u in first + rest]
        else:
            groups = [first, rest] if rest else [first]
        for units in groups:
            if units[0][1] == 0:
                attend(h, units, lambda n: [n], bias[:, DIL_BLK:], in_band[:, DIL_BLK:])
            else:
                attend(h, units, lambda n: [n - 1, n], bias, in_band)


def _dil_attn(qkv, rel_bias, bucket, dist, perm, group, dil, band):
    b, seq, _ = qkv.shape
    ngroups = len(DIL_PATTERNS)
    kern = functools.partial(_dil_attn_kernel, dil=dil, band=band, scale=DIL_HEAD_DIM ** -0.5, group=group)
    full = lambda a: pl.BlockSpec(a.shape, lambda i: (0, 0))

    def in_blk(c):
        return pl.BlockSpec((1, seq, DIL_OUT), lambda i: (i, 0, c * ngroups + group))

    out_shape = (b, DIL_HEADS_PER_GROUP, seq, DIL_HEAD_DIM)
    out_blk = pl.BlockSpec((1,) + out_shape[1:], lambda i: (i, 0, 0, 0))
    stage = [pltpu.VMEM((seq, DIL_OUT), BF16)] * 3 if dil > 1 else []
    return pl.pallas_call(
        kern,
        grid=(b,),
        in_specs=[pl.BlockSpec(memory_space=pltpu.SMEM), full(bucket), full(dist), full(perm),
                  in_blk(0), in_blk(1), in_blk(2)],
        out_specs=[out_blk, out_blk],
        out_shape=[jax.ShapeDtypeStruct(out_shape, F32)] * 2,
        scratch_shapes=[pltpu.VMEM((DIL_HEADS_PER_GROUP, DIL_BLK, 2 * DIL_BLK), F32)] + stage,
        compiler_params=_params("arbitrary"),
        name="dil_attn",
    )(rel_bias, bucket, dist, perm, qkv, qkv, qkv)


def _dil_combine_kernel(o0, o1, o2, l0, l1, l2, out_ref):
    for h in range(DIL_HEADS_PER_GROUP):
        a0, a1, a2 = l0[0, h], l1[0, h], l2[0, h]
        m = jnp.maximum(jnp.maximum(a0, a1), a2)
        e0, e1, e2 = jnp.exp(a0 - m), jnp.exp(a1 - m), jnp.exp(a2 - m)
        inv = 1.0 / (e0 + e1 + e2)
        o = (e0 * inv) * o0[0, h] + (e1 * inv) * o1[0, h] + (e2 * inv) * o2[0, h]
        out_ref[:, h * DIL_HEAD_DIM:(h + 1) * DIL_HEAD_DIM] = o.astype(out_ref.dtype)


def _dil_combine(outs, lses, ts=1024):
    b, nh, seq, hd = outs[0].shape
    nseq = seq // ts
    blk = pl.BlockSpec((1, nh, ts, hd), lambda i, j: (i, 0, j, 0))
    return pl.pallas_call(
        _dil_combine_kernel,
        grid=(b, nseq),
        in_specs=[blk] * 6,
        out_specs=pl.BlockSpec((ts, nh * hd), lambda i, j: (i * nseq + j, 0)),
        out_shape=jax.ShapeDtypeStruct((b * seq, nh * hd), BF16),
        compiler_params=_params("parallel", "parallel"),
        name="dil_combine",
    )(*outs, *lses)


def _merge_kernel(x_ref, oa_ref, ob_ref, oc_ref, wga_ref, wgb_ref, wgc_ref, p_ref, wo_ref, gb_ref, o_ref, xb_ref):
    j = pl.program_id(1)

    @pl.when(j == 0)
    def _():
        x = x_ref[...]
        xb_ref[...] = x.astype(BF16)
        o_ref[...] = ALPHA * x

    xb = xb_ref[...]
    o_sgu = MLA_OUT + SGU_WIDTH
    merged = jax.nn.sigmoid(_dot_nt(xb, wga_ref[...])) * _dot(oa_ref[...], p_ref[:MLA_OUT, :])
    merged += jax.nn.sigmoid(_dot_nt(xb, wgb_ref[...])) * _dot(ob_ref[...], p_ref[MLA_OUT:o_sgu, :])
    merged += jax.nn.sigmoid(_dot_nt(xb, wgc_ref[...])) * _dot(oc_ref[...], p_ref[o_sgu:, :])
    o_ref[...] += _dot(merged.astype(BF16), wo_ref[...])

    @pl.when(j == pl.num_programs(1) - 1)
    def _():
        o_ref[...] = _ln_rows(o_ref[...], gb_ref[0:1, :], gb_ref[1:2, :])


def _merge(x, oa, ob, oc, wg, layer, col0, p, wo, gb, tm=512, tn=MERGE_TN):
    t, d = x.shape
    nj = d // tn
    rows = lambda a: pl.BlockSpec((tm, a.shape[1]), lambda i, j: (i, 0))
    gate = lambda c: _row_block(layer, tn, d, lambda i, j: col0 + c * d + j * tn)
    return pl.pallas_call(
        _merge_kernel,
        grid=(t // tm, nj),
        in_specs=[rows(x), rows(oa), rows(ob), rows(oc), gate(0), gate(1), gate(2),
                  pl.BlockSpec((None, p.shape[1], tn), lambda i, j: (layer, 0, j)),
                  pl.BlockSpec((None, tn, d), lambda i, j: (layer, j, 0)),
                  pl.BlockSpec((2, d), lambda i, j: (0, 0))],
        out_specs=pl.BlockSpec((tm, d), lambda i, j: (i, 0)),
        out_shape=jax.ShapeDtypeStruct((t, d), F32),
        scratch_shapes=[pltpu.VMEM((tm, d), BF16)],
        compiler_params=_params("parallel", "arbitrary"),
        name="merge_out_ln",
    )(x, oa, ob, oc, wg, wg, wg, p, wo, gb)


def _xa_kernel(x_ref, kv_ref, wq_ref, wo_ref, g_ref, b_ref, o_ref):
    x = x_ref[...]
    q = _dot(x.astype(BF16), wq_ref[...]).astype(BF16)
    heads = []
    for h in range(XA_HEADS):
        cs = slice(h * XA_HEAD_DIM, (h + 1) * XA_HEAD_DIM)
        s = _dot_nt(q[:, cs], kv_ref[0, :, cs]) * (XA_HEAD_DIM ** -0.5)
        m = jnp.max(s, axis=-1, keepdims=True)
        e = jnp.exp(s - m)
        p = (e * (1.0 / jnp.sum(e, axis=-1, keepdims=True))).astype(BF16)
        heads.append(_dot(p, kv_ref[0, :, XA_WIDTH + h * XA_HEAD_DIM:XA_WIDTH + (h + 1) * XA_HEAD_DIM]))
    o = jnp.concatenate(heads, axis=-1).astype(BF16)
    o_ref[...] = _ln_rows(ALPHA * x + _dot(o, wo_ref[...]), g_ref[...], b_ref[...])


def _xa(x, kv, wq, wo, layer, g, b, seq, tm=512):
    t, d = x.shape
    nseq = seq // tm
    full = lambda a: pl.BlockSpec(a.shape, lambda i: (0,) * a.ndim)
    of_layer = lambda a: pl.BlockSpec((None,) + a.shape[1:], lambda i: (layer, 0, 0))
    return pl.pallas_call(
        _xa_kernel,
        grid=(t // tm,),
        in_specs=[pl.BlockSpec((tm, d), lambda i: (i, 0)),
                  pl.BlockSpec((1,) + kv.shape[1:], lambda i: (i // nseq, 0, 0)),
                  of_layer(wq), of_layer(wo), full(g), full(b)],
        out_specs=pl.BlockSpec((tm, d), lambda i: (i, 0)),
        out_shape=jax.ShapeDtypeStruct((t, d), F32),
        compiler_params=_params("parallel"),
        name="xattn_ln",
    )(x, kv, wq, wo, g, b)


def _rope_tables128(seq):
    inv = ROPE_THETA ** (-jnp.arange(0, ROPE_DIM, 2, dtype=F32) / ROPE_DIM)
    ang = jnp.arange(seq, dtype=F32)[:, None] * inv[None, :]
    z = jnp.zeros((seq, 128 - ROPE_DIM), F32)
    cos, sin = jnp.cos(ang), jnp.sin(ang)
    return jnp.concatenate([cos, cos, z], axis=1), jnp.concatenate([sin, sin, z], axis=1)


def _rot_cols(w):
    half = w.shape[-1] // 2
    return jnp.concatenate([-w[..., half:], w[..., :half]], axis=-1)


def _pad_last(w, width):
    return jnp.pad(w, [(0, 0)] * (w.ndim - 1) + [(0, width - w.shape[-1])])


def _t5_bucket(dist):
    exact = REL_BUCKETS // 2
    df = jnp.maximum(dist, 1).astype(F32)
    large = exact + (jnp.log(df / exact) / math.log(REL_MAX_DIST / exact) * (REL_BUCKETS - exact)).astype(jnp.int32)
    large = jnp.minimum(large, REL_BUCKETS - 1)
    return jnp.where(dist < exact, dist, large)


def _dil_tables(dil):
    pos = np.concatenate([(tok + stride * np.arange(size)) // dil for _, tok, stride, size in _dil_pieces(dil, 0, 0)])
    key_pos = np.concatenate([pos, pos + DIL_BLK])
    dist = jnp.asarray(pos[:, None] + DIL_BLK - key_pos[None, :], jnp.int32)
    return dist, _t5_bucket(jnp.maximum(dist, 0) * dil).astype(jnp.int32)


def _dil_perm():
    a = np.arange(DIL_CHUNK)
    perm = np.zeros((DIL_CHUNK, DIL_CHUNK), np.float32)
    perm[a, (a % DIL_RES) * DIL_RES + a // DIL_RES] = 1.0
    return jnp.asarray(perm, BF16)


def kernel(x, mem, ln_g, ln_b, ffn_wg, ffn_wu, ffn_wd, w_in, mla_q_norm, mla_kv_norm, mla_w_uq, mla_w_ukv,
           sgu_ln_g, sgu_ln_b, sgu_ws, sgu_bs, rel_bias, w_branch, w_out, xa_wq, xa_wkv, xa_wo):
    batch, seq, d = x.shape
    t = batch * seq
    mem_len = mem.shape[1]
    cos128, sin128 = _rope_tables128(seq)
    h = x.reshape(t, d)
    mem2 = mem.reshape(batch * mem_len, d)
    bf = lambda a: a.astype(BF16)
    row = lambda a: a.reshape(1, -1)

    o_cq = 0
    o_ckv = o_cq + Q_LORA
    o_kpe = o_ckv + KV_LORA
    o_z = o_kpe + ROPE_DIM
    o_qkv = o_z + 2 * SGU_WIDTH
    o_gate = o_qkv + 3 * DIL_HEADS * DIL_HEAD_DIM

    wg_b, wu_b, wd_b = bf(ffn_wg), bf(ffn_wu), bf(ffn_wd)
    w_in_t = jnp.swapaxes(w_in, 1, 2)
    w_wide = bf(w_in_t)
    w_head = jnp.swapaxes(w_in_t[:, :o_z], 1, 2)
    c_qkv, c_gate, c_sgu = o_qkv, o_gate, o_z
    w_branch_b, w_out_b = bf(w_branch), bf(w_out)
    xa_wq_b, xa_wkv_b, xa_wo_b = bf(xa_wq), bf(xa_wkv), bf(xa_wo)

    def ffn(h, l, i, k):
        return _ffn(h, wg_b, wu_b, wd_b, jnp.stack([ln_g[l, k], ln_b[l, k]]), l, i)

    dil_tables = [_dil_tables(dil) for _, dil in DIL_PATTERNS]
    dil_perm = _dil_perm()

    for l in range(DEPTH):
        h = ffn(h, l, 0, 0)

        wl = w_head[l]
        w_kpe = wl[:, o_kpe:o_z]
        w1 = bf(jnp.concatenate([wl[:, o_cq:o_kpe], _pad_last(w_kpe, 128), _pad_last(_rot_cols(w_kpe), 128)], axis=1))
        wuq = mla_w_uq[l].reshape(Q_LORA, MLA_HEADS, NOPE_DIM + ROPE_DIM)
        wqn = bf(wuq[:, :, :NOPE_DIM].reshape(Q_LORA, -1))
        wqr = bf(_pad_last(wuq[:, :, NOPE_DIM:], 128).reshape(Q_LORA, -1))
        wqrr = bf(_pad_last(_rot_cols(wuq[:, :, NOPE_DIM:]), 128).reshape(Q_LORA, -1))
        wukv = mla_w_ukv[l].reshape(KV_LORA, MLA_HEADS, NOPE_DIM + MLA_V_DIM)
        wk = bf(wukv[:, :, :NOPE_DIM].reshape(KV_LORA, -1))
        wv = bf(wukv[:, :, NOPE_DIM:].reshape(KV_LORA, -1))
        q, k, v = _mla_proj(h, w1, row(mla_q_norm[l]), row(mla_kv_norm[l]), wqn, wqr, wqrr, wk, wv, cos128, sin128, seq)
        hq = MLA_HEADS * MLA_QK_PAD
        o_a = _mla_attn(q.reshape(batch, seq, hq), k.reshape(batch, seq, hq), v.reshape(batch, seq, MLA_OUT))
        o_a = o_a.reshape(t, MLA_OUT)

        bs_b = jnp.broadcast_to(sgu_bs[l][:, :, None], (SGU_GROUPS, SGU_CHUNK, SGU_GROUP_DIM))
        o_b = _sgu(h, w_wide, l, c_sgu, row(sgu_ln_g[l]), row(sgu_ln_b[l]), sgu_ws[l], bs_b)

        qkv = _proj(h, w_wide, l, c_qkv, o_gate - o_qkv, BF16, tm=1024, tn=DIL_HEADS * DIL_HEAD_DIM,
                    w_transposed=True)
        qkv = qkv.reshape(batch, seq, qkv.shape[1])
        outs, lses = [], []
        for gi, (window, dil) in enumerate(DIL_PATTERNS):
            dist, bucket = dil_tables[gi]
            o_g, l_g = _dil_attn(qkv, rel_bias, bucket, dist, dil_perm, gi, dil, window // dil)
            outs.append(o_g)
            lses.append(l_g)
        o_c = _dil_combine(outs, lses)

        h = _merge(h, o_a, o_b, o_c, w_wide, l, c_gate, w_branch_b, w_out_b, jnp.stack([ln_g[l, 1], ln_b[l, 1]]))

        kv = _proj(mem2, xa_wkv_b, l, 0, 2 * XA_WIDTH, BF16, tm=1024, tn=2 * XA_WIDTH)
        kv = kv.reshape(batch, mem_len, 2 * XA_WIDTH)
        h = _xa(h, kv, xa_wq_b, xa_wo_b, l, row(ln_g[l, 2]), row(ln_b[l, 2]), seq)

        h = ffn(h, l, 1, 3)
    return h.reshape(batch, seq, d)
```

```python
import functools
import math

import jax
import jax.numpy as jnp
import numpy as np
from jax import lax
from jax.experimental import pallas as pl
from jax.experimental.pallas import tpu as pltpu

F32 = jnp.float32
BF16 = jnp.bfloat16

D_MODEL = 2048
DEPTH = 2
D_FF = 5632
MLA_HEADS = 8
Q_LORA = 512
KV_LORA = 512
NOPE_DIM = 128
ROPE_DIM = 64
MLA_V_DIM = 128
MLA_QK_PAD = 256
ROPE_THETA = 10000.0
SGU_CHUNK = 128
SGU_GROUPS = 4
SGU_GROUP_DIM = 128
SGU_WIDTH = SGU_GROUPS * SGU_GROUP_DIM
DIL_PATTERNS = ((128, 1), (512, 4), (2048, 16))
DIL_HEADS_PER_GROUP = 4
DIL_HEAD_DIM = 128
DIL_HEADS = DIL_HEADS_PER_GROUP * len(DIL_PATTERNS)
DIL_OUT = DIL_HEADS_PER_GROUP * DIL_HEAD_DIM
DIL_BLK = 128
DIL_RES = 16
DIL_CHUNK = DIL_RES * DIL_RES
REL_BUCKETS = 32
REL_MAX_DIST = 2048
XA_HEADS = 4
XA_HEAD_DIM = 128
XA_WIDTH = XA_HEADS * XA_HEAD_DIM
MLA_OUT = MLA_HEADS * MLA_V_DIM
ALPHA = (2 * DEPTH) ** 0.25
LN_EPS = 1e-5
RMS_EPS = 1e-6
NEG_INF = -1e30

VMEM_LIMIT_BYTES = 56 * 1024 * 1024
FFN_TF = 512
MERGE_TN = 512
BF16_SUBLANE_TILE = 16
def _params(*semantics):
    return pltpu.CompilerParams(dimension_semantics=semantics, vmem_limit_bytes=VMEM_LIMIT_BYTES)


def _ln_rows(y, g, b):
    mu = jnp.mean(y, axis=-1, keepdims=True)
    yc = y - mu
    var = jnp.mean(yc * yc, axis=-1, keepdims=True)
    return yc * lax.rsqrt(var + LN_EPS) * g + b


def _rms_rows(y, g):
    return y * lax.rsqrt(jnp.mean(y * y, axis=-1, keepdims=True) + RMS_EPS) * g


def _row_block(layer, th, k, row_of):
    return pl.BlockSpec((pl.Squeezed(), pl.Element(th), pl.Element(k)),
                        lambda *ids: (layer, pl.multiple_of(row_of(*ids), BF16_SUBLANE_TILE), 0))


def _dot(a, b):
    return jnp.dot(a, b, preferred_element_type=F32)


def _dot_nt(a, b):
    return lax.dot_general(a, b, (((1,), (1,)), ((), ())), preferred_element_type=F32)


def _ffn_kernel(x_hbm, wg_ref, wu_ref, wd_ref, gb_ref, o_ref, xbuf, xb_ref, sem, *, tm):
    i = pl.program_id(0)
    f = pl.program_id(1)

    def x_copy(tile):
        return pltpu.make_async_copy(x_hbm.at[pl.ds(tile * tm, tm), :], xbuf, sem)

    @pl.when(f == 0)
    def _():
        @pl.when(i == 0)
        def _():
            x_copy(0).start()

        x_copy(i).wait()
        x = xbuf[...]
        xb_ref[...] = x.astype(BF16)
        o_ref[...] = ALPHA * x

        @pl.when(i + 1 < pl.num_programs(0))
        def _():
            x_copy(i + 1).start()

    xb = xb_ref[...]
    gate = _dot(xb, wg_ref[...])
    up = _dot(xb, wu_ref[...])
    h = (0.5 * (gate * jax.nn.sigmoid(gate)) * up).astype(BF16)
    o_ref[...] += _dot(h, wd_ref[...])

    @pl.when(f == pl.num_programs(1) - 1)
    def _():
        o_ref[...] = _ln_rows(o_ref[...], gb_ref[0:1, :], gb_ref[1:2, :])


def _ffn(x, wg, wu, wd, gb, layer, half, tm=1024, tf=FFN_TF):
    t, d = x.shape
    return pl.pallas_call(
        functools.partial(_ffn_kernel, tm=tm),
        grid=(t // tm, wg.shape[3] // tf),
        in_specs=[
            pl.BlockSpec(memory_space=pl.ANY),
            pl.BlockSpec((None, None, d, tf), lambda i, f: (layer, half, 0, f)),
            pl.BlockSpec((None, None, d, tf), lambda i, f: (layer, half, 0, f)),
            pl.BlockSpec((None, None, tf, d), lambda i, f: (layer, half, f, 0)),
            pl.BlockSpec((2, d), lambda i, f: (0, 0)),
        ],
        out_specs=pl.BlockSpec((tm, d), lambda i, f: (i, 0)),
        out_shape=jax.ShapeDtypeStruct((t, d), F32),
        scratch_shapes=[pltpu.VMEM((tm, d), F32), pltpu.VMEM((tm, d), BF16), pltpu.SemaphoreType.DMA(())],
        compiler_params=_params("arbitrary", "arbitrary"),
        name="ffn_swiglu_ln",
    )(x, wg, wu, wd, gb)


def _proj_kernel(x_ref, w_ref, o_ref, *, w_transposed):
    dot = _dot_nt if w_transposed else _dot
    o_ref[...] = dot(x_ref[...].astype(BF16), w_ref[...]).astype(o_ref.dtype)


def _proj(x, w, layer, col0, n, out_dtype, tm, tn, w_transposed=False):
    t, k = x.shape
    tm = min(tm, t)
    assert t % tm == 0 and n % tn == 0
    if w_transposed:
        w_spec = _row_block(layer, tn, k, lambda i, j: col0 + j * tn)
    else:
        assert col0 % tn == 0
        w_spec = pl.BlockSpec((None, k, tn), lambda i, j: (layer, 0, col0 // tn + j))
    return pl.pallas_call(
        functools.partial(_proj_kernel, w_transposed=w_transposed),
        grid=(t // tm, n // tn),
        in_specs=[pl.BlockSpec((tm, k), lambda i, j: (i, 0)), w_spec],
        out_specs=pl.BlockSpec((tm, tn), lambda i, j: (i, j)),
        out_shape=jax.ShapeDtypeStruct((t, n), out_dtype),
        compiler_params=_params("parallel", "arbitrary"),
        name="proj",
    )(x, w)


def _mla_proj_kernel(x_ref, w1_ref, qn_ref, kvn_ref, wqn_ref, wqr_ref, wqrr_ref, wk_ref, wv_ref,
                     cos_ref, sin_ref, q_ref, k_ref, v_ref):
    xb = x_ref[...].astype(BF16)
    c = _dot(xb, w1_ref[...])
    cos = cos_ref[...]
    sin = sin_ref[...]
    cq = _rms_rows(c[:, :Q_LORA], qn_ref[...]).astype(BF16)
    ckv = _rms_rows(c[:, Q_LORA:Q_LORA + KV_LORA], kvn_ref[...]).astype(BF16)
    off = Q_LORA + KV_LORA
    k_rope = (c[:, off:off + 128] * cos + c[:, off + 128:off + 256] * sin).astype(BF16)
    q_nope = _dot(cq, wqn_ref[...])
    q_pe = _dot(cq, wqr_ref[...])
    q_pe_rot = _dot(cq, wqrr_ref[...])
    k_nope = _dot(ckv, wk_ref[...])
    v_ref[...] = _dot(ckv, wv_ref[...]).astype(BF16)
    for h in range(MLA_HEADS):
        lo = h * MLA_QK_PAD
        hs = slice(h * 128, (h + 1) * 128)
        q_ref[:, lo:lo + 128] = q_nope[:, hs].astype(BF16)
        q_ref[:, lo + 128:lo + 256] = (q_pe[:, hs] * cos + q_pe_rot[:, hs] * sin).astype(BF16)
        k_ref[:, lo:lo + 128] = k_nope[:, hs].astype(BF16)
        k_ref[:, lo + 128:lo + 256] = k_rope


def _mla_proj(x, w1, qn, kvn, wqn, wqr, wqrr, wk, wv, cos128, sin128, seq, tm=512):
    t, d = x.shape
    nseq = seq // tm
    full = lambda a: pl.BlockSpec(a.shape, lambda i: (0,) * a.ndim)
    hq = MLA_HEADS * MLA_QK_PAD
    return pl.pallas_call(
        _mla_proj_kernel,
        grid=(t // tm,),
        in_specs=[
            pl.BlockSpec((tm, d), lambda i: (i, 0)),
            full(w1), full(qn), full(kvn), full(wqn), full(wqr), full(wqrr), full(wk), full(wv),
            pl.BlockSpec((tm, 128), lambda i: (i % nseq, 0)),
            pl.BlockSpec((tm, 128), lambda i: (i % nseq, 0)),
        ],
        out_specs=[
            pl.BlockSpec((tm, hq), lambda i: (i, 0)),
            pl.BlockSpec((tm, hq), lambda i: (i, 0)),
            pl.BlockSpec((tm, MLA_OUT), lambda i: (i, 0)),
        ],
        out_shape=[
            jax.ShapeDtypeStruct((t, hq), BF16),
            jax.ShapeDtypeStruct((t, hq), BF16),
            jax.ShapeDtypeStruct((t, MLA_OUT), BF16),
        ],
        compiler_params=_params("parallel"),
        name="mla_proj",
    )(x, w1, qn, kvn, wqn, wqr, wqrr, wk, wv, cos128, sin128)


def _mla_attn_kernel(q_ref, k_ref, v_ref, o_ref, *, tq, scale):
    seq = q_ref.shape[1]
    c = scale * math.log2(math.e)
    causal = (lax.broadcasted_iota(jnp.int32, (tq, tq), 1) <= lax.broadcasted_iota(jnp.int32, (tq, tq), 0))
    for qi in range(seq // tq):
        lo = qi * tq
        rows = slice(lo, lo + tq)
        q = q_ref[0, rows, :]
        s_d = jnp.where(causal, _dot_nt(q, k_ref[0, rows, :]), NEG_INF)
        m = jnp.max(s_d, axis=-1, keepdims=True)
        if qi > 0:
            s_p = _dot_nt(q, k_ref[0, :lo, :])
            m = jnp.maximum(m, jnp.max(s_p, axis=-1, keepdims=True))
            e_p = jnp.exp2((s_p - m) * c)
        e_d = jnp.exp2((s_d - m) * c)
        den = jnp.sum(e_d, axis=-1, keepdims=True)
        if qi > 0:
            den = den + jnp.sum(e_p, axis=-1, keepdims=True)
        inv = 1.0 / den
        o = _dot((e_d * inv).astype(BF16), v_ref[0, rows, :])
        if qi > 0:
            o = o + _dot((e_p * inv).astype(BF16), v_ref[0, :lo, :])
        o_ref[0, rows, :] = o.astype(o_ref.dtype)


def _mla_attn(q, k, v, tq=512):
    b, s, _ = q.shape
    kern = functools.partial(_mla_attn_kernel, tq=tq, scale=(NOPE_DIM + ROPE_DIM) ** -0.5)
    return pl.pallas_call(
        kern,
        grid=(b, MLA_HEADS),
        in_specs=[
            pl.BlockSpec((1, s, MLA_QK_PAD), lambda i, h: (i, 0, h)),
            pl.BlockSpec((1, s, MLA_QK_PAD), lambda i, h: (i, 0, h)),
            pl.BlockSpec((1, s, MLA_V_DIM), lambda i, h: (i, 0, h)),
        ],
        out_specs=pl.BlockSpec((1, s, MLA_V_DIM), lambda i, h: (i, 0, h)),
        out_shape=jax.ShapeDtypeStruct((b, s, MLA_OUT), BF16),
        compiler_params=_params("parallel", "parallel"),
        name="mla_attn",
    )(q, k, v)


def _sgu_kernel(x_ref, wu_ref, wv_ref, g_ref, b_ref, ws_ref, bs_ref, o_ref):
    tm = x_ref.shape[0]
    xb = x_ref[...].astype(BF16)
    gelu = lambda z: 0.5 * z * (1.0 + lax.erf(z * (2.0 ** -0.5)))
    u = gelu(_dot_nt(xb, wu_ref[...]))
    vn = _ln_rows(gelu(_dot_nt(xb, wv_ref[...])), g_ref[...], b_ref[...]).astype(BF16)
    row = lax.broadcasted_iota(jnp.int32, (SGU_CHUNK, SGU_CHUNK), 0)
    col = lax.broadcasted_iota(jnp.int32, (SGU_CHUNK, SGU_CHUNK), 1)
    for gi in range(SGU_GROUPS):
        w = jnp.where(col <= row, ws_ref[gi], 0.0).astype(BF16)
        bias = bs_ref[gi]
        cs = slice(gi * SGU_GROUP_DIM, (gi + 1) * SGU_GROUP_DIM)
        for ci in range(tm // SGU_CHUNK):
            rs = slice(ci * SGU_CHUNK, (ci + 1) * SGU_CHUNK)
            mixed = _dot(w, vn[rs, cs]) + bias
            o_ref[rs, cs] = (u[rs, cs] * mixed).astype(o_ref.dtype)


def _sgu(x, w, layer, col0, g, b, ws, bs_b, tm=1024):
    t, d = x.shape
    full = lambda a: pl.BlockSpec(a.shape, lambda i: (0,) * a.ndim)
    half = lambda c: _row_block(layer, SGU_WIDTH, d, lambda i: col0 + c * SGU_WIDTH)
    return pl.pallas_call(
        _sgu_kernel,
        grid=(t // tm,),
        in_specs=[pl.BlockSpec((tm, d), lambda i: (i, 0)), half(0), half(1), full(g), full(b), full(ws), full(bs_b)],
        out_specs=pl.BlockSpec((tm, SGU_WIDTH), lambda i: (i, 0)),
        out_shape=jax.ShapeDtypeStruct((t, SGU_WIDTH), BF16),
        compiler_params=_params("parallel"),
        name="sgu",
    )(x, w, w, g, b, ws, bs_b)


def _dil_pieces(dil, r, n):
    if dil == 1:
        return [(n * DIL_BLK, n * DIL_BLK, 1, DIL_BLK)]
    span = DIL_BLK * dil // DIL_CHUNK
    pieces = []
    for c in range(n * span, (n + 1) * span):
        for u in range(DIL_RES // dil):
            res = u * dil + r
            pieces.append((c * DIL_CHUNK + res * DIL_RES, c * DIL_CHUNK + res, DIL_RES, DIL_RES))
    return pieces


def _dil_attn_kernel(rel_ref, bucket_ref, dist_ref, perm_ref, q_ref, k_ref, v_ref, o_ref, l_ref, bias_ref, *stage,
                     dil, band, scale, group):
    seq = q_ref.shape[1]
    sub = seq // dil

    @pl.when(pl.program_id(0) == 0)
    def _():
        bucket = bucket_ref[...]
        for h in range(DIL_HEADS_PER_GROUP):
            acc = jnp.zeros(bucket.shape, F32)
            for bk in range(REL_BUCKETS):
                acc = jnp.where(bucket == bk, rel_ref[bk, group * DIL_HEADS_PER_GROUP + h], acc)
            bias_ref[h] = acc

    if dil > 1:
        perm = perm_ref[...]
        for src, dst in zip((q_ref, k_ref, v_ref), stage):
            for c in range(seq // DIL_CHUNK):
                cs = slice(c * DIL_CHUNK, (c + 1) * DIL_CHUNK)
                dst[cs, :] = _dot(perm, src[0, cs, :]).astype(BF16)

    def load(idx, h, r, blocks):
        cols = slice(h * DIL_HEAD_DIM, (h + 1) * DIL_HEAD_DIM)
        if dil == 1:
            return (q_ref, k_ref, v_ref)[idx][0, blocks[0] * DIL_BLK:(blocks[-1] + 1) * DIL_BLK, cols]
        parts = [stage[idx][row:row + size, cols] for n in blocks for row, _, _, size in _dil_pieces(dil, r, n)]
        return jnp.concatenate(parts, axis=0)

    dist = dist_ref[...]
    in_band = (dist >= 0) & (dist <= band)

    def attend(h, units, key_blocks, bias, mask):
        q = jnp.stack([load(0, h, r, [n]) for r, n in units])
        k = jnp.stack([load(1, h, r, key_blocks(n)) for r, n in units])
        v = jnp.stack([load(2, h, r, key_blocks(n)) for r, n in units])
        s = lax.dot_general(q, k, (((2,), (2,)), ((0,), (0,))), preferred_element_type=F32) * scale + bias
        s = jnp.where(mask, s, NEG_INF)
        m = jnp.max(s, axis=-1, keepdims=True)
        e = jnp.exp(s - m)
        den = jnp.sum(e, axis=-1, keepdims=True)
        p = (e * (1.0 / den)).astype(BF16)
        o = lax.dot_general(p, v, (((2,), (1,)), ((0,), (0,))), preferred_element_type=F32)
        lse = jnp.broadcast_to(m + jnp.log(den), o.shape)
        for u, (r, n) in enumerate(units):
            at = 0
            for _, tok, stride, size in _dil_pieces(dil, r, n):
                rows = pl.ds(tok, size) if stride == 1 else pl.ds(tok, size, stride=stride)
                o_ref[0, h, rows, :] = o[u, at:at + size]
                l_ref[0, h, rows, :] = lse[u, at:at + size]
                at += size

    nblk = sub // DIL_BLK
    for h in range(DIL_HEADS_PER_GROUP):
        bias = bias_ref[h]
        first = [(r, 0) for r in range(dil)]
        rest = [(r, n) for r in range(dil) for n in range(1, nblk)]
        if dil == 1:
            groups = [[u] for u in first + rest]
        else:
            groups = [first, rest] if rest else [first]
        for units in groups:
            if units[0][1] == 0:
                attend(h, units, lambda n: [n], bias[:, DIL_BLK:], in_band[:, DIL_BLK:])
            else:
                attend(h, units, lambda n: [n - 1, n], bias, in_band)


def _dil_attn(qkv, rel_bias, bucket, dist, perm, group, dil, band):
    b, seq, _ = qkv.shape
    ngroups = len(DIL_PATTERNS)
    kern = functools.partial(_dil_attn_kernel, dil=dil, band=band, scale=DIL_HEAD_DIM ** -0.5, group=group)
    full = lambda a: pl.BlockSpec(a.shape, lambda i: (0, 0))

    def in_blk(c):
        return pl.BlockSpec((1, seq, DIL_OUT), lambda i: (i, 0, c * ngroups + group))

    out_shape = (b, DIL_HEADS_PER_GROUP, seq, DIL_HEAD_DIM)
    out_blk = pl.BlockSpec((1,) + out_shape[1:], lambda i: (i, 0, 0, 0))
    stage = [pltpu.VMEM((seq, DIL_OUT), BF16)] * 3 if dil > 1 else []
    return pl.pallas_call(
        kern,
        grid=(b,),
        in_specs=[pl.BlockSpec(memory_space=pltpu.SMEM), full(bucket), full(dist), full(perm),
                  in_blk(0), in_blk(1), in_blk(2)],
        out_specs=[out_blk, out_blk],
        out_shape=[jax.ShapeDtypeStruct(out_shape, F32)] * 2,
        scratch_shapes=[pltpu.VMEM((DIL_HEADS_PER_GROUP, DIL_BLK, 2 * DIL_BLK), F32)] + stage,
        compiler_params=_params("arbitrary"),
        name="dil_attn",
    )(rel_bias, bucket, dist, perm, qkv, qkv, qkv)


def _dil_combine_kernel(o0, o1, o2, l0, l1, l2, out_ref):
    for h in range(DIL_HEADS_PER_GROUP):
        a0, a1, a2 = l0[0, h], l1[0, h], l2[0, h]
        m = jnp.maximum(jnp.maximum(a0, a1), a2)
        e0, e1, e2 = jnp.exp(a0 - m), jnp.exp(a1 - m), jnp.exp(a2 - m)
        inv = 1.0 / (e0 + e1 + e2)
        o = (e0 * inv) * o0[0, h] + (e1 * inv) * o1[0, h] + (e2 * inv) * o2[0, h]
        out_ref[:, h * DIL_HEAD_DIM:(h + 1) * DIL_HEAD_DIM] = o.astype(out_ref.dtype)


def _dil_combine(outs, lses, ts=1024):
    b, nh, seq, hd = outs[0].shape
    nseq = seq // ts
    blk = pl.BlockSpec((1, nh, ts, hd), lambda i, j: (i, 0, j, 0))
    return pl.pallas_call(
        _dil_combine_kernel,
        grid=(b, nseq),
        in_specs=[blk] * 6,
        out_specs=pl.BlockSpec((ts, nh * hd), lambda i, j: (i * nseq + j, 0)),
        out_shape=jax.ShapeDtypeStruct((b * seq, nh * hd), BF16),
        compiler_params=_params("parallel", "parallel"),
        name="dil_combine",
    )(*outs, *lses)


def _merge_kernel(x_ref, oa_ref, ob_ref, oc_ref, wga_ref, wgb_ref, wgc_ref, p_ref, wo_ref, gb_ref, o_ref, xb_ref):
    j = pl.program_id(1)

    @pl.when(j == 0)
    def _():
        x = x_ref[...]
        xb_ref[...] = x.astype(BF16)
        o_ref[...] = ALPHA * x

    xb = xb_ref[...]
    o_sgu = MLA_OUT + SGU_WIDTH
    merged = jax.nn.sigmoid(_dot_nt(xb, wga_ref[...])) * _dot(oa_ref[...], p_ref[:MLA_OUT, :])
    merged += jax.nn.sigmoid(_dot_nt(xb, wgb_ref[...])) * _dot(ob_ref[...], p_ref[MLA_OUT:o_sgu, :])
    merged += jax.nn.sigmoid(_dot_nt(xb, wgc_ref[...])) * _dot(oc_ref[...], p_ref[o_sgu:, :])
    o_ref[...] += _dot(merged.astype(BF16), wo_ref[...])

    @pl.when(j == pl.num_programs(1) - 1)
    def _():
        o_ref[...] = _ln_rows(o_ref[...], gb_ref[0:1, :], gb_ref[1:2, :])


def _merge(x, oa, ob, oc, wg, layer, col0, p, wo, gb, tm=512, tn=MERGE_TN):
    t, d = x.shape
    nj = d // tn
    rows = lambda a: pl.BlockSpec((tm, a.shape[1]), lambda i, j: (i, 0))
    gate = lambda c: _row_block(layer, tn, d, lambda i, j: col0 + c * d + j * tn)
    return pl.pallas_call(
        _merge_kernel,
        grid=(t // tm, nj),
        in_specs=[rows(x), rows(oa), rows(ob), rows(oc), gate(0), gate(1), gate(2),
                  pl.BlockSpec((None, p.shape[1], tn), lambda i, j: (layer, 0, j)),
                  pl.BlockSpec((None, tn, d), lambda i, j: (layer, j, 0)),
                  pl.BlockSpec((2, d), lambda i, j: (0, 0))],
        out_specs=pl.BlockSpec((tm, d), lambda i, j: (i, 0)),
        out_shape=jax.ShapeDtypeStruct((t, d), F32),
        scratch_shapes=[pltpu.VMEM((tm, d), BF16)],
        compiler_params=_params("parallel", "arbitrary"),
        name="merge_out_ln",
    )(x, oa, ob, oc, wg, wg, wg, p, wo, gb)


def _xa_kernel(x_ref, kv_ref, wq_ref, wo_ref, g_ref, b_ref, o_ref):
    x = x_ref[...]
    q = _dot(x.astype(BF16), wq_ref[...]).astype(BF16)
    heads = []
    for h in range(XA_HEADS):
        cs = slice(h * XA_HEAD_DIM, (h + 1) * XA_HEAD_DIM)
        s = _dot_nt(q[:, cs], kv_ref[0, :, cs]) * (XA_HEAD_DIM ** -0.5)
        m = jnp.max(s, axis=-1, keepdims=True)
        e = jnp.exp(s - m)
        p = (e * (1.0 / jnp.sum(e, axis=-1, keepdims=True))).astype(BF16)
        heads.append(_dot(p, kv_ref[0, :, XA_WIDTH + h * XA_HEAD_DIM:XA_WIDTH + (h + 1) * XA_HEAD_DIM]))
    o = jnp.concatenate(heads, axis=-1).astype(BF16)
    o_ref[...] = _ln_rows(ALPHA * x + _dot(o, wo_ref[...]), g_ref[...], b_ref[...])


def _xa(x, kv, wq, wo, layer, g, b, seq, tm=1024):
    t, d = x.shape
    nseq = seq // tm
    full = lambda a: pl.BlockSpec(a.shape, lambda i: (0,) * a.ndim)
    of_layer = lambda a: pl.BlockSpec((None,) + a.shape[1:], lambda i: (layer, 0, 0))
    return pl.pallas_call(
        _xa_kernel,
        grid=(t // tm,),
        in_specs=[pl.BlockSpec((tm, d), lambda i: (i, 0)),
                  pl.BlockSpec((1,) + kv.shape[1:], lambda i: (i // nseq, 0, 0)),
                  of_layer(wq), of_layer(wo), full(g), full(b)],
        out_specs=pl.BlockSpec((tm, d), lambda i: (i, 0)),
        out_shape=jax.ShapeDtypeStruct((t, d), F32),
        compiler_params=_params("parallel"),
        name="xattn_ln",
    )(x, kv, wq, wo, g, b)


def _rope_tables128(seq):
    inv = ROPE_THETA ** (-jnp.arange(0, ROPE_DIM, 2, dtype=F32) / ROPE_DIM)
    ang = jnp.arange(seq, dtype=F32)[:, None] * inv[None, :]
    z = jnp.zeros((seq, 128 - ROPE_DIM), F32)
    cos, sin = jnp.cos(ang), jnp.sin(ang)
    return jnp.concatenate([cos, cos, z], axis=1), jnp.concatenate([sin, sin, z], axis=1)


def _rot_cols(w):
    half = w.shape[-1] // 2
    return jnp.concatenate([-w[..., half:], w[..., :half]], axis=-1)


def _pad_last(w, width):
    return jnp.pad(w, [(0, 0)] * (w.ndim - 1) + [(0, width - w.shape[-1])])


def _t5_bucket(dist):
    exact = REL_BUCKETS // 2
    df = jnp.maximum(dist, 1).astype(F32)
    large = exact + (jnp.log(df / exact) / math.log(REL_MAX_DIST / exact) * (REL_BUCKETS - exact)).astype(jnp.int32)
    large = jnp.minimum(large, REL_BUCKETS - 1)
    return jnp.where(dist < exact, dist, large)


def _dil_tables(dil):
    pos = np.concatenate([(tok + stride * np.arange(size)) // dil for _, tok, stride, size in _dil_pieces(dil, 0, 0)])
    key_pos = np.concatenate([pos, pos + DIL_BLK])
    dist = jnp.asarray(pos[:, None] + DIL_BLK - key_pos[None, :], jnp.int32)
    return dist, _t5_bucket(jnp.maximum(dist, 0) * dil).astype(jnp.int32)


def _dil_perm():
    a = np.arange(DIL_CHUNK)
    perm = np.zeros((DIL_CHUNK, DIL_CHUNK), np.float32)
    perm[a, (a % DIL_RES) * DIL_RES + a // DIL_RES] = 1.0
    return jnp.asarray(perm, BF16)


def kernel(x, mem, ln_g, ln_b, ffn_wg, ffn_wu, ffn_wd, w_in, mla_q_norm, mla_kv_norm, mla_w_uq, mla_w_ukv,
           sgu_ln_g, sgu_ln_b, sgu_ws, sgu_bs, rel_bias, w_branch, w_out, xa_wq, xa_wkv, xa_wo):
    batch, seq, d = x.shape
    t = batch * seq
    mem_len = mem.shape[1]
    cos128, sin128 = _rope_tables128(seq)
    h = x.reshape(t, d)
    mem2 = mem.reshape(batch * mem_len, d)
    bf = lambda a: a.astype(BF16)
    row = lambda a: a.reshape(1, -1)

    o_cq = 0
    o_ckv = o_cq + Q_LORA
    o_kpe = o_ckv + KV_LORA
    o_z = o_kpe + ROPE_DIM
    o_qkv = o_z + 2 * SGU_WIDTH
    o_gate = o_qkv + 3 * DIL_HEADS * DIL_HEAD_DIM

    wg_b, wu_b, wd_b = bf(ffn_wg), bf(ffn_wu), bf(ffn_wd)
    w_in_t = jnp.swapaxes(w_in, 1, 2)
    w_wide = bf(w_in_t)
    w_head = jnp.swapaxes(w_in_t[:, :o_z], 1, 2)
    c_qkv, c_gate, c_sgu = o_qkv, o_gate, o_z
    w_branch_b, w_out_b = bf(w_branch), bf(w_out)
    xa_wq_b, xa_wkv_b, xa_wo_b = bf(xa_wq), bf(xa_wkv), bf(xa_wo)

    def ffn(h, l, i, k):
        return _ffn(h, wg_b, wu_b, wd_b, jnp.stack([ln_g[l, k], ln_b[l, k]]), l, i)

    dil_tables = [_dil_tables(dil) for _, dil in DIL_PATTERNS]
    dil_perm = _dil_perm()

    for l in range(DEPTH):
        h = ffn(h, l, 0, 0)

        wl = w_head[l]
        w_kpe = wl[:, o_kpe:o_z]
        w1 = bf(jnp.concatenate([wl[:, o_cq:o_kpe], _pad_last(w_kpe, 128), _pad_last(_rot_cols(w_kpe), 128)], axis=1))
        wuq = mla_w_uq[l].reshape(Q_LORA, MLA_HEADS, NOPE_DIM + ROPE_DIM)
        wqn = bf(wuq[:, :, :NOPE_DIM].reshape(Q_LORA, -1))
        wqr = bf(_pad_last(wuq[:, :, NOPE_DIM:], 128).reshape(Q_LORA, -1))
        wqrr = bf(_pad_last(_rot_cols(wuq[:, :, NOPE_DIM:]), 128).reshape(Q_LORA, -1))
        wukv = mla_w_ukv[l].reshape(KV_LORA, MLA_HEADS, NOPE_DIM + MLA_V_DIM)
        wk = bf(wukv[:, :, :NOPE_DIM].reshape(KV_LORA, -1))
        wv = bf(wukv[:, :, NOPE_DIM:].reshape(KV_LORA, -1))
        q, k, v = _mla_proj(h, w1, row(mla_q_norm[l]), row(mla_kv_norm[l]), wqn, wqr, wqrr, wk, wv, cos128, sin128, seq)
        hq = MLA_HEADS * MLA_QK_PAD
        o_a = _mla_attn(q.reshape(batch, seq, hq), k.reshape(batch, seq, hq), v.reshape(batch, seq, MLA_OUT))
        o_a = o_a.reshape(t, MLA_OUT)

        bs_b = jnp.broadcast_to(sgu_bs[l][:, :, None], (SGU_GROUPS, SGU_CHUNK, SGU_GROUP_DIM))
        o_b = _sgu(h, w_wide, l, c_sgu, row(sgu_ln_g[l]), row(sgu_ln_b[l]), sgu_ws[l], bs_b)

        qkv = _proj(h, w_wide, l, c_qkv, o_gate - o_qkv, BF16, tm=1024, tn=DIL_HEADS * DIL_HEAD_DIM,
                    w_transposed=True)
        qkv = qkv.reshape(batch, seq, qkv.shape[1])
        outs, lses = [], []
        for gi, (window, dil) in enumerate(DIL_PATTERNS):
            dist, bucket = dil_tables[gi]
            o_g, l_g = _dil_attn(qkv, rel_bias, bucket, dist, dil_perm, gi, dil, window // dil)
            outs.append(o_g)
            lses.append(l_g)
        o_c = _dil_combine(outs, lses)

        h = _merge(h, o_a, o_b, o_c, w_wide, l, c_gate, w_branch_b, w_out_b, jnp.stack([ln_g[l, 1], ln_b[l, 1]]))

        kv = _proj(mem2, xa_wkv_b, l, 0, 2 * XA_WIDTH, BF16, tm=1024, tn=2 * XA_WIDTH)
        kv = kv.reshape(batch, mem_len, 2 * XA_WIDTH)
        h = _xa(h, kv, xa_wq_b, xa_wo_b, l, row(ln_g[l, 2]), row(ln_b[l, 2]), seq)

        h = ffn(h, l, 1, 3)
    return h.reshape(batch, seq, d)
```
